```python
import jax
import jax.numpy as jnp
from jax import lax
import numpy as np

D_MODEL = 2048
BATCH = 16
SEQ = 256
DEPTH = 4
DEC_BATCH = 8
DEC_SEQ = 1024
PAST_LEN = 512

GRID_W = 64
HD = 64
H_RW = 12
C_RW = H_RW * HD
H_NA = 12
C_NA = H_NA * HD
C_CV = D_MODEL - C_RW - C_NA
R_W = 64
R_A = 64
R_G = 128
RW_IN = 3 * C_RW + 2 * R_W + 2 * R_A + R_G
N_IN = RW_IN + 3 * C_NA + 2 * C_CV
RW_SPLITS = (C_RW, 2 * C_RW, 3 * C_RW, 3 * C_RW + 2 * R_W, 3 * C_RW + 2 * R_W + 2 * R_A)
NA_KH_MAX = 8
NA_KW = 16
NA_QCB = 16
NA_BAND = 32
Q_BLOCK = 128
CONV_K = 31
N_GROUPS = 4
EXP_PER_GROUP = 8
N_EXPERTS = N_GROUPS * EXP_PER_GROUP
TOP_K = 2
D_EXPERT = 1024
MOE_BLOCK = 128
NORM_EPS = 1e-6
LN_EPS = 1e-5
RW_GN_EPS = 64e-5
NEG = -1e30

kernel_name = 'hybrid_rwkv7_natten_conformer_hmoe_step'


def _rmsnorm(x, g):
    xf = x.astype(jnp.float32)
    y = xf * lax.rsqrt(jnp.mean(xf * xf, -1, keepdims=True) + NORM_EPS)
    return (y * g.astype(jnp.float32)).astype(x.dtype)


def _layernorm(x, g, b, eps):
    xf = x.astype(jnp.float32)
    mu = jnp.mean(xf, -1, keepdims=True)
    var = jnp.mean(jnp.square(xf - mu), -1, keepdims=True)
    return ((xf - mu) * lax.rsqrt(var + eps) * g + b).astype(x.dtype)


def _adaln(cond, w_mod, b_mod):
    m = jax.nn.silu(cond) @ w_mod + b_mod
    return jnp.split(m[..., None, :], 6, axis=-1)


def _token_shift(p, mu):
    prev = jnp.pad(p[:, :-1], ((0, 0), (1, 0), (0, 0)))
    nxt = jnp.pad(p[:, 1:], ((0, 0), (0, 1), (0, 0)))
    return p + mu[0] * (prev - p) + mu[1] * (nxt - p)


def _rwkv_scan(r, decay, k, v, aa, bb, S0):
    def dirs(x):
        x = jnp.stack([x[0], x[1][:, ::-1]])
        return jnp.moveaxis(x, 2, 0).astype(jnp.float32)
    xs = tuple(dirs(t) for t in (jnp.stack([r, r]), decay, k, jnp.stack([v, v]), aa, bb))

    def step(S, inp):
        rt, wt, kt, vt, at, bt = inp
        sa = jnp.einsum('dbhvk,dbhk->dbhv', S, at)
        S = S * wt[..., None, :] + sa[..., None] * bt[..., None, :] + vt[..., None] * kt[..., None, :]
        return S, jnp.einsum('dbhvk,dbhk->dbhv', S, rt)

    S_fin, ys = lax.scan(step, S0.astype(jnp.float32), xs)
    ys = jnp.moveaxis(ys, 0, 2)
    ys = jnp.stack([ys[0], ys[1][:, ::-1]])
    return ys.astype(r.dtype), S_fin.astype(S0.dtype)


def _rwkv_mix(p, S0, P):
    B, T, _ = p.shape
    p = _token_shift(p, P['rw_shift'])
    r, k, v, wd, ad, gd = jnp.split(p, RW_SPLITS, axis=-1)
    wd = wd.reshape(B, T, 2, R_W)
    ad = ad.reshape(B, T, 2, R_A)
    wl = P['rw_w0'][:, None, None, :] + jnp.einsum('btdr,drc->dbtc', jnp.tanh(wd), P['rw_w2'])
    w_log = -jax.nn.softplus(-wl.astype(jnp.float32)) - 0.5
    decay = jnp.exp(-jnp.exp(w_log))
    a = jax.nn.sigmoid(P['rw_a0'][:, None, None, :] + jnp.einsum('btdr,drc->dbtc', ad, P['rw_a2']))
    g = jax.nn.sigmoid(gd) @ P['rw_g2']
    kmod = k[None] * (1 + (a - 1) * P['rw_ka'])

    def heads(x):
        return x.reshape(x.shape[:-1] + (H_RW, HD))
    r_h, v_h, a_h, kmod_h, dec_h = heads(r), heads(v), heads(a), heads(kmod), heads(decay)
    kkf = heads(k * P['rw_kk']).astype(jnp.float32)
    kk_h = (kkf * lax.rsqrt(jnp.maximum(jnp.sum(kkf * kkf, -1, keepdims=True), 1e-24))).astype(k.dtype)
    aa = jnp.broadcast_to(-kk_h, kmod_h.shape)
    bb = kk_h * a_h
    ys, S_fin = _rwkv_scan(r_h, dec_h, kmod_h, v_h, aa, bb, S0)
    y = _layernorm(ys[0] + ys[1], P['rw_ln_g'].reshape(H_RW, HD), P['rw_ln_b'].reshape(H_RW, HD), RW_GN_EPS)
    bonus = jnp.sum(jnp.sum(r_h[None] * kmod_h * P['rw_rk'], -1, keepdims=True) * v_h[None], 0)
    return (y + bonus).reshape(B, T, C_RW) * g, S_fin


def _attn_ctx(q, k, v):
    B, L, H, N = q.shape
    qb = jnp.moveaxis(q.reshape(B, L // Q_BLOCK, Q_BLOCK, H, N), 1, 0)

    def blk(qi):
        s = jnp.einsum('bqhn,bkhn->bhqk', qi, k).astype(jnp.float32) * (N ** -0.5)
        pr = jax.nn.softmax(s, -1).astype(v.dtype)
        return jnp.einsum('bhqk,bkhn->bqhn', pr, v)
    o = lax.map(blk, qb)
    return jnp.moveaxis(o, 0, 1).reshape(B, L, H, N)


def _natten_lat(q, k, v, kc, vc, rpb):
    B, T, H, N = q.shape
    rows = T // GRID_W
    kh = min(NA_KH_MAX, rows)
    qg = q.reshape(B, rows, GRID_W, H, N)
    kg = k.reshape(B, rows, GRID_W, H, N)
    vg = v.reshape(B, rows, GRID_W, H, N)
    ncb = GRID_W // NA_QCB
    band0 = np.clip(np.arange(ncb) * NA_QCB - (NA_BAND - NA_QCB) // 2, 0, GRID_W - NA_BAND)
    band_cols = band0[:, None] + np.arange(NA_BAND)[None]
    qcols = np.arange(GRID_W).reshape(ncb, NA_QCB)
    win0 = np.clip(qcols - NA_KW // 2, 0, GRID_W - NA_KW)
    col_ok = (band_cols[:, None, :] >= win0[..., None]) & (band_cols[:, None, :] < win0[..., None] + NA_KW)
    dc_idx = np.clip(band_cols[:, None, :] - qcols[..., None] + NA_KW - 1, 0, 2 * NA_KW - 2)
    rpb_c = rpb[:, :, dc_idx]
    col_ok = jnp.asarray(col_ok)[:, :, None, :]
    scale = N ** -0.5
    nloc = kh * NA_BAND

    def row(i):
        si = jnp.clip(i - kh // 2, 0, rows - kh)
        kr = lax.dynamic_slice_in_dim(kg, si, kh, axis=1)[:, :, band_cols]
        vr = lax.dynamic_slice_in_dim(vg, si, kh, axis=1)[:, :, band_cols]
        qr = lax.dynamic_index_in_dim(qg, i, axis=1, keepdims=False).reshape(B, ncb, NA_QCB, H, N)
        dr_idx = si + jnp.arange(kh) - i + NA_KH_MAX - 1
        bias = jnp.transpose(jnp.take(rpb_c, dr_idx, axis=1), (0, 2, 3, 1, 4))
        s_loc = jnp.einsum('bcqhn,brckhn->bhcqrk', qr, kr).astype(jnp.float32) * scale + bias[None].astype(jnp.float32)
        s_loc = jnp.where(col_ok, s_loc, NEG)
        s_ctx = jnp.einsum('bcqhn,bshn->bhcqs', qr, kc).astype(jnp.float32) * scale
        s = jnp.concatenate([s_loc.reshape(B, H, ncb, NA_QCB, nloc), s_ctx], -1)
        pr = jax.nn.softmax(s, -1).astype(v.dtype)
        o = (jnp.einsum('bhcqrk,brckhn->bcqhn', pr[..., :nloc].reshape(B, H, ncb, NA_QCB, kh, NA_BAND), vr)
             + jnp.einsum('bhcqs,bshn->bcqhn', pr[..., nloc:], vc))
        return o.reshape(B, GRID_W, H, N)

    out = lax.map(row, jnp.arange(rows))
    return jnp.moveaxis(out, 0, 1).reshape(B, T, H, N)


def _conv_module(p, P):
    val, gate = jnp.split(p, 2, axis=-1)
    u = val * jax.nn.sigmoid(gate)
    u = lax.conv_general_dilated(u, P['cv_dw'][:, None, :], (1,), [(CONV_K // 2, CONV_K // 2)],
                                 dimension_numbers=('NWC', 'WIO', 'NWC'), feature_group_count=C_CV)
    u = _layernorm(u + P['cv_dw_b'], P['cv_ln_g'], P['cv_ln_b'], LN_EPS)
    return jax.nn.silu(u)


def _expert_dispatch(x, eid, w, w13, w2):
    n, D = x.shape
    a_tot = n * TOP_K
    n_blk = -(-a_tot // MOE_BLOCK) + N_EXPERTS
    rows = n_blk * MOE_BLOCK
    flat_e = eid.reshape(-1)
    flat_t = jnp.arange(a_tot, dtype=jnp.int32) // TOP_K
    flat_w = w.reshape(-1)
    order = jnp.argsort(flat_e, stable=True)
    se = flat_e[order]
    counts = jnp.bincount(flat_e, length=N_EXPERTS)
    pcounts = (counts + MOE_BLOCK - 1) // MOE_BLOCK * MOE_BLOCK
    starts = jnp.cumsum(counts) - counts
    pends = jnp.cumsum(pcounts)
    pstarts = pends - pcounts
    dest = pstarts[se] + jnp.arange(a_tot, dtype=jnp.int32) - starts[se]
    row_tok = jnp.zeros((rows,), jnp.int32).at[dest].set(flat_t[order])
    row_w = jnp.zeros((rows,), flat_w.dtype).at[dest].set(flat_w[order])
    blk_e = jnp.minimum(jnp.searchsorted(pends, jnp.arange(n_blk) * MOE_BLOCK, side='right'), N_EXPERTS - 1)
    xb = x[row_tok].reshape(n_blk, MOE_BLOCK, D)

    def expert_block(args):
        xe, e = args
        gt, up = jnp.split(xe @ w13[e], 2, axis=-1)
        return (jax.nn.silu(gt) * up) @ w2[e]
    yb = lax.map(expert_block, (xb, blk_e)).reshape(rows, D)
    return jnp.zeros_like(x).at[row_tok].add(yb * row_w[:, None].astype(x.dtype))


def _hmoe(h, P):
    B, T, D = h.shape
    x = h.reshape(-1, D)
    n = x.shape[0]
    lg = (x @ P['rt_group']).astype(jnp.float32)
    grp = jnp.argmax(lg, -1).astype(jnp.int32)
    gate_g = jnp.take_along_axis(jax.nn.softmax(lg, -1), grp[:, None], -1)
    le = (x @ P['rt_expert']).astype(jnp.float32).reshape(n, N_GROUPS, EXP_PER_GROUP)
    le = jnp.take_along_axis(le, grp[:, None, None], 1)[:, 0]
    top_l, top_i = lax.top_k(le, TOP_K)
    w = gate_g * jax.nn.softmax(top_l, -1)
    eid = grp[:, None] * EXP_PER_GROUP + top_i.astype(jnp.int32)
    return _expert_dispatch(x, eid, w, P['ex_w13'], P['ex_w2']).reshape(B, T, D)


def _project(x, cond, P):
    mods = _adaln(cond, P['w_mod'], P['b_mod'])
    h = _rmsnorm(x, P['norm1_g']) * (1 + mods[1]) + mods[0]
    p = h @ P['w_in']
    p_rw, p_na, p_cv = jnp.split(p, [RW_IN, RW_IN + 3 * C_NA], axis=-1)
    return mods, p_rw, p_na, p_cv


def _finish(x, mods, y_rw, y_na, y_cv, P):
    x = x + mods[2] * (jnp.concatenate([y_rw, y_na, y_cv], -1) @ P['w_out'])
    h = _rmsnorm(x, P['norm2_g']) * (1 + mods[4]) + mods[3]
    return x + mods[5] * _hmoe(h, P)


def _layer_ctx(x, c_ctx, P):
    B, L, _ = x.shape
    mods, p_rw, p_na, p_cv = _project(x, c_ctx, P)
    y_rw, S_fin = _rwkv_mix(p_rw, jnp.zeros((2, B, H_RW, HD, HD), x.dtype), P)
    q, k, v = (t.reshape(B, L, H_NA, HD) for t in jnp.split(p_na, 3, axis=-1))
    y_na = _attn_ctx(q, k, v).reshape(B, L, C_NA)
    y_cv = _conv_module(p_cv, P)
    return _finish(x, mods, y_rw, y_na, y_cv, P), k, v, jnp.moveaxis(S_fin, 0, 1)


def _layer_lat(x, cond, kc, vc, S_ctx, P):
    B, T, _ = x.shape
    mods, p_rw, p_na, p_cv = _project(x, cond, P)
    y_rw, _ = _rwkv_mix(p_rw, jnp.moveaxis(S_ctx, 1, 0), P)
    q, k, v = (t.reshape(B, T, H_NA, HD) for t in jnp.split(p_na, 3, axis=-1))
    y_na = _natten_lat(q, k, v, kc, vc, P['na_rpb']).reshape(B, T, C_NA)
    y_cv = _conv_module(p_cv, P)
    return _finish(x, mods, y_rw, y_na, y_cv, P)


def setup_inputs(seed: int = 0) -> dict:
    key = jax.random.key(seed)
    ks = iter(jax.random.split(key, 48))
    L, D = DEPTH, D_MODEL

    def nrm(shape, s):
        return s * jax.random.normal(next(ks), shape, jnp.float32)

    def unif(shape, lo, hi):
        return jax.random.uniform(next(ks), shape, jnp.float32, lo, hi)

    return {
        'x_prompt': nrm((BATCH, SEQ, D), 1.0),
        'x_sample': nrm((DEC_BATCH, DEC_SEQ, D), 1.0),
        'cache_k': nrm((DEC_BATCH, L, PAST_LEN, H_NA, HD), 1.0),
        'cache_v': nrm((DEC_BATCH, L, PAST_LEN, H_NA, HD), 1.0),
        'state_rwkv': nrm((DEC_BATCH, L, 2, H_RW, HD, HD), 1.0),
        'c': nrm((DEC_BATCH, D), 1.0),
        'c_ctx': nrm((D,), 1.0),
        'norm1_g': 1.0 + nrm((L, D), 0.02),
        'w_mod': nrm((L, D, 6 * D), 0.5 * D ** -0.5),
        'b_mod': nrm((L, 6 * D), 0.02),
        'w_in': nrm((L, D, N_IN), D ** -0.5),
        'rw_shift': unif((L, 2, RW_IN), 0.0, 0.5),
        'rw_w0': unif((L, 2, C_RW), -6.0, 0.0),
        'rw_w2': nrm((L, 2, R_W, C_RW), 0.5 * R_W ** -0.5),
        'rw_a0': nrm((L, 2, C_RW), 0.1),
        'rw_a2': nrm((L, 2, R_A, C_RW), 0.5 * R_A ** -0.5),
        'rw_g2': nrm((L, R_G, C_RW), R_G ** -0.5),
        'rw_kk': 0.85 + nrm((L, C_RW), 0.05),
        'rw_ka': 1.0 + nrm((L, C_RW), 0.05),
        'rw_rk': nrm((L, H_RW, HD), 0.1),
        'rw_ln_g': 1.0 + nrm((L, C_RW), 0.02),
        'rw_ln_b': nrm((L, C_RW), 0.02),
        'na_rpb': nrm((L, H_NA, 2 * NA_KH_MAX - 1, 2 * NA_KW - 1), 0.1),
        'cv_dw': nrm((L, CONV_K, C_CV), CONV_K ** -0.5),
        'cv_dw_b': nrm((L, C_CV), 0.02),
        'cv_ln_g': 1.0 + nrm((L, C_CV), 0.02),
        'cv_ln_b': nrm((L, C_CV), 0.02),
        'w_out': nrm((L, D, D), D ** -0.5),
        'norm2_g': 1.0 + nrm((L, D), 0.02),
        'rt_group': nrm((L, D, N_GROUPS), D ** -0.5),
        'rt_expert': nrm((L, D, N_EXPERTS), D ** -0.5),
        'ex_w13': nrm((L, N_EXPERTS, D, 2 * D_EXPERT), D ** -0.5),
        'ex_w2': nrm((L, N_EXPERTS, D_EXPERT, D), D_EXPERT ** -0.5),
        'final_g': 1.0 + nrm((D,), 0.02),
    }


def reference(x_prompt, x_sample, cache_k, cache_v, state_rwkv, c, c_ctx, norm1_g, w_mod, b_mod, w_in,
              rw_shift, rw_w0, rw_w2, rw_a0, rw_a2, rw_g2, rw_kk, rw_ka, rw_rk, rw_ln_g, rw_ln_b, na_rpb,
              cv_dw, cv_dw_b, cv_ln_g, cv_ln_b, w_out, norm2_g, rt_group, rt_expert, ex_w13, ex_w2, final_g):
    xp, xs = x_prompt, x_sample
    ks_out, vs_out, ss_out = [], [], []
    for l in range(DEPTH):
        P = dict(norm1_g=norm1_g[l], w_mod=w_mod[l], b_mod=b_mod[l], w_in=w_in[l], rw_shift=rw_shift[l],
                 rw_w0=rw_w0[l], rw_w2=rw_w2[l], rw_a0=rw_a0[l], rw_a2=rw_a2[l], rw_g2=rw_g2[l],
                 rw_kk=rw_kk[l], rw_ka=rw_ka[l], rw_rk=rw_rk[l], rw_ln_g=rw_ln_g[l], rw_ln_b=rw_ln_b[l],
                 na_rpb=na_rpb[l], cv_dw=cv_dw[l], cv_dw_b=cv_dw_b[l], cv_ln_g=cv_ln_g[l], cv_ln_b=cv_ln_b[l],
                 w_out=w_out[l], norm2_g=norm2_g[l], rt_group=rt_group[l], rt_expert=rt_expert[l],
                 ex_w13=ex_w13[l], ex_w2=ex_w2[l])
        xp, k_l, v_l, s_l = _layer_ctx(xp, c_ctx, P)
        ks_out.append(k_l)
        vs_out.append(v_l)
        ss_out.append(s_l)
        xs = _layer_lat(xs, c, cache_k[:, l], cache_v[:, l], state_rwkv[:, l], P)
    y_prompt = _rmsnorm(xp, final_g)
    y_sample = _rmsnorm(xs, final_g)
    new_cache_k = jnp.stack(ks_out, axis=1)
    new_cache_v = jnp.stack(vs_out, axis=1)
    new_state_rwkv = jnp.stack(ss_out, axis=1)
    return (y_prompt, y_sample, new_cache_k, new_cache_v, new_state_rwkv)
```

```python
import functools

import jax
import jax.numpy as jnp
import numpy as np
from jax import lax
from jax.experimental import pallas as pl
from jax.experimental.pallas import tpu as pltpu

D_MODEL = 2048
BATCH = 16
SEQ = 256
DEPTH = 4
DEC_BATCH = 8
DEC_SEQ = 1024
PAST_LEN = 512

GRID_W = 64
HD = 64
H_RW = 12
C_RW = H_RW * HD
H_NA = 12
C_NA = H_NA * HD
C_CV = D_MODEL - C_RW - C_NA
R_W = 64
R_A = 64
R_G = 128
RW_IN = 3 * C_RW + 2 * R_W + 2 * R_A + R_G
N_IN = RW_IN + 3 * C_NA + 2 * C_CV
RW_SPLITS = (C_RW, 2 * C_RW, 3 * C_RW, 3 * C_RW + 2 * R_W, 3 * C_RW + 2 * R_W + 2 * R_A)
NA_KH_MAX = 8
NA_KW = 16
NA_QCB = 16
NA_BAND = 32
Q_BLOCK = 128
CONV_K = 31
N_GROUPS = 4
EXP_PER_GROUP = 8
N_EXPERTS = N_GROUPS * EXP_PER_GROUP
TOP_K = 2
D_EXPERT = 1024
NORM_EPS = 1e-6
LN_EPS = 1e-5
RW_GN_EPS = 64e-5
NEG = -1e30

SEG = 1024
N_CTX_TOK = BATCH * SEQ
N_LAT_TOK = DEC_BATCH * DEC_SEQ
N_TOK = N_CTX_TOK + N_LAT_TOK
N_SEG = N_TOK // SEG
N_CTX_SEG = N_CTX_TOK // SEG
RT_PAD = 128
MOE_BM = 256
MOE_NBLK = (N_TOK * TOP_K) // MOE_BM + N_EXPERTS
VMEM_LIMIT = 48 * 1024 * 1024


PROJ_TN = 512
PROJ_ROWS = 128


def _proj_kernel(x_ref, mods_ref, g_ref, w_ref, o_ref, h_scr):
    @pl.when(pl.program_id(1) == 0)
    def _():
        shift = mods_ref[0, 0:1, :]
        scale = 1.0 + mods_ref[0, 1:2, :]
        g = g_ref[...]

        def body(r, _):
            rows = pl.ds(pl.multiple_of(r * PROJ_ROWS, PROJ_ROWS), PROJ_ROWS)
            x = x_ref[rows, :]
            y = x * lax.rsqrt(jnp.mean(x * x, -1, keepdims=True) + NORM_EPS)
            h_scr[rows, :] = ((y * g) * scale + shift).astype(jnp.bfloat16)
            return 0
        lax.fori_loop(0, SEG // PROJ_ROWS, body, 0)

    o_ref[...] = jnp.dot(h_scr[...], w_ref[...], preferred_element_type=jnp.float32)


def _project(x, mods, g, w_bf16):
    n_col = pl.cdiv(N_IN, PROJ_TN)
    return pl.pallas_call(
        _proj_kernel,
        grid=(N_SEG, n_col),
        in_specs=[
            pl.BlockSpec((SEG, D_MODEL), lambda i, j: (i, 0)),
            pl.BlockSpec((1, 6, D_MODEL), lambda i, j: (i, 0, 0)),
            pl.BlockSpec((1, D_MODEL), lambda i, j: (0, 0)),
            pl.BlockSpec((D_MODEL, PROJ_TN), lambda i, j: (0, j)),
        ],
        out_specs=pl.BlockSpec((SEG, PROJ_TN), lambda i, j: (i, j)),
        out_shape=jax.ShapeDtypeStruct((N_TOK, N_IN), jnp.float32),
        scratch_shapes=[pltpu.VMEM((SEG, D_MODEL), jnp.bfloat16)],
        compiler_params=pltpu.CompilerParams(
            dimension_semantics=("arbitrary", "arbitrary"), vmem_limit_bytes=VMEM_LIMIT),
        name="in_proj",
    )(x, mods, g, w_bf16)


OUT_TM = 256


def _out_kernel(y_ref, x_ref, mods_ref, g_ref, w_ref, rt_ref, xo_ref, h_ref, lg_ref):
    y = y_ref[...].astype(jnp.bfloat16)
    x_new = x_ref[...] + mods_ref[0, 2:3, :] * jnp.dot(y, w_ref[...], preferred_element_type=jnp.float32)
    xo_ref[...] = x_new
    n = x_new * lax.rsqrt(jnp.mean(x_new * x_new, -1, keepdims=True) + NORM_EPS)
    h = (n * g_ref[...]) * (1.0 + mods_ref[0, 4:5, :]) + mods_ref[0, 3:4, :]
    h_ref[...] = h.astype(jnp.bfloat16)
    lg_ref[...] = jnp.dot(h, rt_ref[...], preferred_element_type=jnp.float32,
                          precision=lax.Precision.HIGHEST)


def _out_project(ycat, x, mods, g, w_bf16, rt):
    per_seg = SEG // OUT_TM
    return pl.pallas_call(
        _out_kernel,
        grid=(N_TOK // OUT_TM,),
        in_specs=[
            pl.BlockSpec((OUT_TM, D_MODEL), lambda i: (i, 0)),
            pl.BlockSpec((OUT_TM, D_MODEL), lambda i: (i, 0)),
            pl.BlockSpec((1, 6, D_MODEL), lambda i: (i // per_seg, 0, 0)),
            pl.BlockSpec((1, D_MODEL), lambda i: (0, 0)),
            pl.BlockSpec((D_MODEL, D_MODEL), lambda i: (0, 0)),
            pl.BlockSpec((D_MODEL, RT_PAD), lambda i: (0, 0)),
        ],
        out_specs=[
            pl.BlockSpec((OUT_TM, D_MODEL), lambda i: (i, 0)),
            pl.BlockSpec((OUT_TM, D_MODEL), lambda i: (i, 0)),
            pl.BlockSpec((OUT_TM, RT_PAD), lambda i: (i, 0)),
        ],
        out_shape=[
            jax.ShapeDtypeStruct((N_TOK, D_MODEL), jnp.float32),
            jax.ShapeDtypeStruct((N_TOK, D_MODEL), jnp.bfloat16),
            jax.ShapeDtypeStruct((N_TOK, RT_PAD), jnp.float32),
        ],
        compiler_params=pltpu.CompilerParams(
            dimension_semantics=("arbitrary",), vmem_limit_bytes=VMEM_LIMIT),
        name="out_proj",
    )(ycat, x, mods, g, w_bf16, rt)


def _moe_kernel(blk_e_ref, n_used_ref, xb_ref, rw_ref, w13_ref, w2_ref, o_ref):
    i = pl.program_id(0)

    @pl.when(i < n_used_ref[0])
    def _():
        gu = jnp.dot(xb_ref[...], w13_ref[0], preferred_element_type=jnp.float32)
        gt = gu[:, :D_EXPERT]
        up = gu[:, D_EXPERT:]
        a = (gt * jax.nn.sigmoid(gt) * up).astype(jnp.bfloat16)
        o_ref[...] = jnp.dot(a, w2_ref[0], preferred_element_type=jnp.float32) * rw_ref[...]

    @pl.when(i >= n_used_ref[0])
    def _():
        o_ref[...] = jnp.zeros_like(o_ref)


def _moe_blocks(blk_e, n_used, xb, row_w, w13_bf16, w2_bf16):
    grid_spec = pltpu.PrefetchScalarGridSpec(
        num_scalar_prefetch=2,
        grid=(MOE_NBLK,),
        in_specs=[
            pl.BlockSpec((MOE_BM, D_MODEL), lambda i, be, nu: (i, 0)),
            pl.BlockSpec((MOE_BM, 1), lambda i, be, nu: (i, 0)),
            pl.BlockSpec((1, D_MODEL, 2 * D_EXPERT), lambda i, be, nu: (be[i], 0, 0)),
            pl.BlockSpec((1, D_EXPERT, D_MODEL), lambda i, be, nu: (be[i], 0, 0)),
        ],
        out_specs=pl.BlockSpec((MOE_BM, D_MODEL), lambda i, be, nu: (i, 0)),
    )
    return pl.pallas_call(
        _moe_kernel,
        grid_spec=grid_spec,
        out_shape=jax.ShapeDtypeStruct((MOE_NBLK * MOE_BM, D_MODEL), jnp.float32),
        compiler_params=pltpu.CompilerParams(
            dimension_semantics=("arbitrary",), vmem_limit_bytes=VMEM_LIMIT),
        name="moe_blocks",
    )(blk_e, n_used, xb, row_w, w13_bf16, w2_bf16)


def _hmoe(h_bf16, logits, w13_bf16, w2_bf16):
    n = N_TOK
    lg = logits[:, :N_GROUPS]
    grp = jnp.argmax(lg, -1).astype(jnp.int32)
    gate_g = jnp.take_along_axis(jax.nn.softmax(lg, -1), grp[:, None], -1)
    le = logits[:, N_GROUPS:N_GROUPS + N_EXPERTS].reshape(n, N_GROUPS, EXP_PER_GROUP)
    le = jnp.take_along_axis(le, grp[:, None, None], 1)[:, 0]
    top_l, top_i = lax.top_k(le, TOP_K)
    w = gate_g * jax.nn.softmax(top_l, -1)
    eid = grp[:, None] * EXP_PER_GROUP + top_i.astype(jnp.int32)

    a_tot = n * TOP_K
    rows = MOE_NBLK * MOE_BM
    flat_e = eid.reshape(-1)
    flat_t = jnp.arange(a_tot, dtype=jnp.int32) // TOP_K
    flat_w = w.reshape(-1)
    onehot = (flat_e[:, None] == jnp.arange(N_EXPERTS, dtype=jnp.int32)[None]).astype(jnp.int32)
    csum = jnp.cumsum(onehot, axis=0)
    counts = csum[-1]
    rank = jnp.take_along_axis(csum, flat_e[:, None], 1)[:, 0] - 1
    pcounts = (counts + MOE_BM - 1) // MOE_BM * MOE_BM
    pends = jnp.cumsum(pcounts)
    pstarts = pends - pcounts
    dest = pstarts[flat_e] + rank
    row_tok = jnp.zeros((rows,), jnp.int32).at[dest].set(flat_t)
    row_w = jnp.zeros((rows,), jnp.float32).at[dest].set(flat_w)
    blk_start = jnp.arange(MOE_NBLK, dtype=jnp.int32) * MOE_BM
    n_used = (pends[-1] // MOE_BM).astype(jnp.int32)
    blk_e = jnp.searchsorted(pends, blk_start, side='right').astype(jnp.int32)
    last_e = jnp.searchsorted(pends, pends[-1] - 1, side='right').astype(jnp.int32)
    blk_e = jnp.minimum(blk_e, last_e)
    xb = h_bf16[row_tok]
    yb = _moe_blocks(blk_e, n_used.reshape(1), xb, row_w[:, None], w13_bf16, w2_bf16)
    d2 = dest.reshape(n, TOP_K)
    return yb[d2[:, 0]] + yb[d2[:, 1]]


def _layernorm(x, g, b, eps):
    mu = jnp.mean(x, -1, keepdims=True)
    var = jnp.mean(jnp.square(x - mu), -1, keepdims=True)
    return (x - mu) * lax.rsqrt(var + eps) * g + b


def _token_shift(p, mu):
    prev = jnp.pad(p[:, :-1], ((0, 0), (1, 0), (0, 0)))
    nxt = jnp.pad(p[:, 1:], ((0, 0), (0, 1), (0, 0)))
    return p + mu[0] * (prev - p) + mu[1] * (nxt - p)


def _rwkv_scan(r, decay, k, v, aa, bb, S0):
    def dirs(x):
        x = jnp.stack([x[0], x[1][:, ::-1]])
        return jnp.moveaxis(x, 2, 0).astype(jnp.float32)
    xs = tuple(dirs(t) for t in (jnp.stack([r, r]), decay, k, jnp.stack([v, v]), aa, bb))

    def step(S, inp):
        rt, wt, kt, vt, at, bt = inp
        sa = jnp.sum(S * at[..., None, :], -1)
        S = S * wt[..., None, :] + sa[..., None] * bt[..., None, :] + vt[..., None] * kt[..., None, :]
        return S, jnp.sum(S * rt[..., None, :], -1)

    S_fin, ys = lax.scan(step, S0.astype(jnp.float32), xs)
    ys = jnp.moveaxis(ys, 0, 2)
    ys = jnp.stack([ys[0], ys[1][:, ::-1]])
    return ys, S_fin


def _rwkv_mix(p, S0, P):
    B, T, _ = p.shape
    p = _token_shift(p, P['rw_shift'])
    r, k, v, wd, ad, gd = jnp.split(p, RW_SPLITS, axis=-1)
    wd = wd.reshape(B, T, 2, R_W)
    ad = ad.reshape(B, T, 2, R_A)
    wl = P['rw_w0'][:, None, None, :] + jnp.einsum('btdr,drc->dbtc', jnp.tanh(wd), P['rw_w2'])
    w_log = -jax.nn.softplus(-wl) - 0.5
    decay = jnp.exp(-jnp.exp(w_log))
    a = jax.nn.sigmoid(P['rw_a0'][:, None, None, :] + jnp.einsum('btdr,drc->dbtc', ad, P['rw_a2']))
    g = jax.nn.sigmoid(gd) @ P['rw_g2']
    kmod = k[None] * (1 + (a - 1) * P['rw_ka'])

    def heads(x):
        return x.reshape(x.shape[:-1] + (H_RW, HD))
    r_h, v_h, a_h, kmod_h, dec_h = heads(r), heads(v), heads(a), heads(kmod), heads(decay)
    kkf = heads(k * P['rw_kk'])
    kk_h = kkf * lax.rsqrt(jnp.maximum(jnp.sum(kkf * kkf, -1, keepdims=True), 1e-24))
    aa = jnp.broadcast_to(-kk_h, kmod_h.shape)
    bb = kk_h * a_h
    ys, S_fin = _rwkv_scan(r_h, dec_h, kmod_h, v_h, aa, bb, S0)
    y = _layernorm(ys[0] + ys[1], P['rw_ln_g'].reshape(H_RW, HD), P['rw_ln_b'].reshape(H_RW, HD), RW_GN_EPS)
    bonus = jnp.sum(jnp.sum(r_h[None] * kmod_h * P['rw_rk'], -1, keepdims=True) * v_h[None], 0)
    return (y + bonus).reshape(B, T, C_RW) * g, S_fin


def _attn_ctx(q, k, v):
    s = jnp.einsum('bqhn,bkhn->bhqk', q, k) * (HD ** -0.5)
    pr = jax.nn.softmax(s, -1)
    return jnp.einsum('bhqk,bkhn->bqhn', pr, v)


def _natten_lat(q, k, v, kc, vc, rpb):
    B, T, H, N = q.shape
    rows = T // GRID_W
    kh = min(NA_KH_MAX, rows)
    qg = q.reshape(B, rows, GRID_W, H, N)
    kg = k.reshape(B, rows, GRID_W, H, N)
    vg = v.reshape(B, rows, GRID_W, H, N)
    ncb = GRID_W // NA_QCB
    band0 = np.clip(np.arange(ncb) * NA_QCB - (NA_BAND - NA_QCB) // 2, 0, GRID_W - NA_BAND)
    band_cols = band0[:, None] + np.arange(NA_BAND)[None]
    qcols = np.arange(GRID_W).reshape(ncb, NA_QCB)
    win0 = np.clip(qcols - NA_KW // 2, 0, GRID_W - NA_KW)
    col_ok = (band_cols[:, None, :] >= win0[..., None]) & (band_cols[:, None, :] < win0[..., None] + NA_KW)
    dc_idx = np.clip(band_cols[:, None, :] - qcols[..., None] + NA_KW - 1, 0, 2 * NA_KW - 2)
    rpb_c = rpb[:, :, dc_idx]
    col_ok = jnp.asarray(col_ok)[:, :, None, :]
    scale = N ** -0.5
    nloc = kh * NA_BAND

    def row(i):
        si = jnp.clip(i - kh // 2, 0, rows - kh)
        kr = lax.dynamic_slice_in_dim(kg, si, kh, axis=1)[:, :, band_cols]
        vr = lax.dynamic_slice_in_dim(vg, si, kh, axis=1)[:, :, band_cols]
        qr = lax.dynamic_index_in_dim(qg, i, axis=1, keepdims=False).reshape(B, ncb, NA_QCB, H, N)
        dr_idx = si + jnp.arange(kh) - i + NA_KH_MAX - 1
        bias = jnp.transpose(jnp.take(rpb_c, dr_idx, axis=1), (0, 2, 3, 1, 4))
        s_loc = jnp.einsum('bcqhn,brckhn->bhcqrk', qr, kr) * scale + bias[None]
        s_loc = jnp.where(col_ok, s_loc, NEG)
        s_ctx = jnp.einsum('bcqhn,bshn->bhcqs', qr, kc) * scale
        s = jnp.concatenate([s_loc.reshape(B, H, ncb, NA_QCB, nloc), s_ctx], -1)
        pr = jax.nn.softmax(s, -1)
        o = (jnp.einsum('bhcqrk,brckhn->bcqhn', pr[..., :nloc].reshape(B, H, ncb, NA_QCB, kh, NA_BAND), vr)
             + jnp.einsum('bhcqs,bshn->bcqhn', pr[..., nloc:], vc))
        return o.reshape(B, GRID_W, H, N)

    out = lax.map(row, jnp.arange(rows))
    return jnp.moveaxis(out, 0, 1).reshape(B, T, H, N)


def _conv_module(p, P):
    val, gate = jnp.split(p, 2, axis=-1)
    u = val * jax.nn.sigmoid(gate)
    u = lax.conv_general_dilated(u, P['cv_dw'][:, None, :], (1,), [(CONV_K // 2, CONV_K // 2)],
                                 dimension_numbers=('NWC', 'WIO', 'NWC'), feature_group_count=C_CV)
    u = _layernorm(u + P['cv_dw_b'], P['cv_ln_g'], P['cv_ln_b'], LN_EPS)
    return jax.nn.silu(u)


def _mix(p, B, T, S0, kc, vc, P):
    p_rw = p[..., :RW_IN]
    p_na = p[..., RW_IN:RW_IN + 3 * C_NA]
    p_cv = p[..., RW_IN + 3 * C_NA:]
    y_rw, S_fin = _rwkv_mix(p_rw, S0, P)
    q, k, v = (t.reshape(B, T, H_NA, HD) for t in jnp.split(p_na, 3, axis=-1))
    if kc is None:
        y_na = _attn_ctx(q, k, v)
    else:
        y_na = _natten_lat(q, k, v, kc, vc, P['na_rpb'])
    y_cv = _conv_module(p_cv, P)
    return jnp.concatenate([y_rw, y_na.reshape(B, T, C_NA), y_cv], -1), k, v, S_fin


def _rmsnorm_final(x, g):
    return x * lax.rsqrt(jnp.mean(x * x, -1, keepdims=True) + NORM_EPS) * g


def kernel(x_prompt, x_sample, cache_k, cache_v, state_rwkv, c, c_ctx, norm1_g, w_mod, b_mod, w_in,
           rw_shift, rw_w0, rw_w2, rw_a0, rw_a2, rw_g2, rw_kk, rw_ka, rw_rk, rw_ln_g, rw_ln_b, na_rpb,
           cv_dw, cv_dw_b, cv_ln_g, cv_ln_b, w_out, norm2_g, rt_group, rt_expert, ex_w13, ex_w2, final_g):
    x = jnp.concatenate([x_prompt.reshape(N_CTX_TOK, D_MODEL), x_sample.reshape(N_LAT_TOK, D_MODEL)], 0)
    cond = jnp.concatenate([c_ctx[None], c], 0)
    seg_cond = np.concatenate([np.zeros(N_CTX_SEG, np.int32), 1 + np.arange(DEC_BATCH, dtype=np.int32)])
    ks_out, vs_out, ss_out = [], [], []
    for l in range(DEPTH):
        P = dict(rw_shift=rw_shift[l], rw_w0=rw_w0[l], rw_w2=rw_w2[l], rw_a0=rw_a0[l], rw_a2=rw_a2[l],
                 rw_g2=rw_g2[l], rw_kk=rw_kk[l], rw_ka=rw_ka[l], rw_rk=rw_rk[l], rw_ln_g=rw_ln_g[l],
                 rw_ln_b=rw_ln_b[l], na_rpb=na_rpb[l], cv_dw=cv_dw[l], cv_dw_b=cv_dw_b[l],
                 cv_ln_g=cv_ln_g[l], cv_ln_b=cv_ln_b[l])
        m = jax.nn.silu(cond) @ w_mod[l] + b_mod[l]
        mods = m.reshape(1 + DEC_BATCH, 6, D_MODEL)[seg_cond]
        p = _project(x, mods, norm1_g[l][None], w_in[l].astype(jnp.bfloat16))
        p_ctx = p[:N_CTX_TOK].reshape(BATCH, SEQ, N_IN)
        p_lat = p[N_CTX_TOK:].reshape(DEC_BATCH, DEC_SEQ, N_IN)
        y_ctx, k_l, v_l, s_l = _mix(p_ctx, BATCH, SEQ, jnp.zeros((2, BATCH, H_RW, HD, HD), jnp.float32),
                                    None, None, P)
        y_lat, _, _, _ = _mix(p_lat, DEC_BATCH, DEC_SEQ, jnp.moveaxis(state_rwkv[:, l], 1, 0),
                              cache_k[:, l], cache_v[:, l], P)
        ks_out.append(k_l)
        vs_out.append(v_l)
        ss_out.append(jnp.moveaxis(s_l, 0, 1))
        ycat = jnp.concatenate([y_ctx.reshape(N_CTX_TOK, D_MODEL), y_lat.reshape(N_LAT_TOK, D_MODEL)], 0)
        rt = jnp.concatenate([rt_group[l], rt_expert[l],
                              jnp.zeros((D_MODEL, RT_PAD - N_GROUPS - N_EXPERTS), jnp.float32)], 1)
        x_new, h2, logits = _out_project(ycat, x, mods, norm2_g[l][None], w_out[l].astype(jnp.bfloat16), rt)
        moe = _hmoe(h2, logits, ex_w13[l].astype(jnp.bfloat16), ex_w2[l].astype(jnp.bfloat16))
        x = x_new + mods[:, 5].repeat(SEG, axis=0) * moe
    y = _rmsnorm_final(x, final_g)
    y_prompt = y[:N_CTX_TOK].reshape(BATCH, SEQ, D_MODEL)
    y_sample = y[N_CTX_TOK:].reshape(DEC_BATCH, DEC_SEQ, D_MODEL)
    return (y_prompt, y_sample, jnp.stack(ks_out, axis=1), jnp.stack(vs_out, axis=1),
            jnp.stack(ss_out, axis=1))
```

```python
import functools

import jax
import jax.numpy as jnp
import numpy as np
from jax import lax
from jax.experimental import pallas as pl
from jax.experimental.pallas import tpu as pltpu

D_MODEL = 2048
BATCH = 16
SEQ = 256
DEPTH = 4
DEC_BATCH = 8
DEC_SEQ = 1024
PAST_LEN = 512

GRID_W = 64
HD = 64
H_RW = 12
C_RW = H_RW * HD
H_NA = 12
C_NA = H_NA * HD
C_CV = D_MODEL - C_RW - C_NA
R_W = 64
R_A = 64
R_G = 128
RW_IN = 3 * C_RW + 2 * R_W + 2 * R_A + R_G
NA_IN = 3 * C_NA
CV_IN = 2 * C_CV
NA_KH_MAX = 8
NA_KW = 16
CONV_K = 31
N_GROUPS = 4
EXP_PER_GROUP = 8
N_EXPERTS = N_GROUPS * EXP_PER_GROUP
TOP_K = 2
D_EXPERT = 1024
NORM_EPS = 1e-6
LN_EPS = 1e-5
RW_GN_EPS = 64e-5
NEG = -1e30

LANES = 128
N_PAIR = H_RW // 2
SEG = 1024
N_CTX_TOK = BATCH * SEQ
N_LAT_TOK = DEC_BATCH * DEC_SEQ
N_TOK = N_CTX_TOK + N_LAT_TOK
N_SEG = N_TOK // SEG
N_CTX_SEG = N_CTX_TOK // SEG
RT_PAD = LANES
MOE_BM = 256
MOE_NBLK = (N_TOK * TOP_K) // MOE_BM + N_EXPERTS
VMEM_LIMIT = 48 * 1024 * 1024

BF16 = jnp.bfloat16
F32 = jnp.float32


def _dot(a, b):
    return jnp.dot(a, b, preferred_element_type=F32)


def _dot_t(a, b):
    return lax.dot_general(a, b, (((1,), (1,)), ((), ())), preferred_element_type=F32)


def _sigmoid(x):
    return 1.0 / (1.0 + jnp.exp(-x))


def _head_sum(x, m_a):
    s_a = jnp.sum(jnp.where(m_a, x, 0.0), -1, keepdims=True)
    s_b = jnp.sum(jnp.where(m_a, 0.0, x), -1, keepdims=True)
    return jnp.where(m_a, s_a, s_b)


PROJ_ROWS = 128
PROJ_TM = 512
PROJ_RW_TN = RW_IN // 3
PROJ_NA_TN = NA_IN // 3
PROJ_CV_TN = CV_IN // 2
PROJ_STEPS = 8


def _proj_kernel(x_ref, mods_ref, g_ref, wrw_ref, wna_ref, wcv_ref, orw_ref, ona_ref, ocv_ref, h_scr):
    j = pl.program_id(1)

    @pl.when(j == 0)
    def _():
        shift = mods_ref[0, 0:1, :]
        scale = 1.0 + mods_ref[0, 1:2, :]
        g = g_ref[...]

        def body(r, _):
            rows = pl.ds(pl.multiple_of(r * PROJ_ROWS, PROJ_ROWS), PROJ_ROWS)
            x = x_ref[rows, :]
            y = x * lax.rsqrt(jnp.mean(x * x, -1, keepdims=True) + NORM_EPS)
            h_scr[rows, :] = ((y * g) * scale + shift).astype(BF16)
            return 0
        lax.fori_loop(0, PROJ_TM // PROJ_ROWS, body, 0)

    @pl.when(j < 3)
    def _():
        orw_ref[...] = _dot(h_scr[...], wrw_ref[...])

    @pl.when((j >= 3) & (j < 6))
    def _():
        ona_ref[...] = _dot(h_scr[...], wna_ref[...])

    @pl.when(j >= 6)
    def _():
        ocv_ref[...] = _dot(h_scr[...], wcv_ref[...])


def _project(x, mods, g, w_rw, w_na, w_cv):
    def c_rw(j):
        return jnp.minimum(j, 2)

    def c_na(j):
        return jnp.clip(j - 3, 0, 2)

    def c_cv(j):
        return jnp.clip(j - 6, 0, 1)

    return pl.pallas_call(
        _proj_kernel,
        grid=(N_TOK // PROJ_TM, PROJ_STEPS),
        in_specs=[
            pl.BlockSpec((PROJ_TM, D_MODEL), lambda i, j: (i, 0)),
            pl.BlockSpec((1, 6, D_MODEL), lambda i, j: (i // (SEG // PROJ_TM), 0, 0)),
            pl.BlockSpec((1, D_MODEL), lambda i, j: (0, 0)),
            pl.BlockSpec((D_MODEL, PROJ_RW_TN), lambda i, j: (0, c_rw(j))),
            pl.BlockSpec((D_MODEL, PROJ_NA_TN), lambda i, j: (0, c_na(j))),
            pl.BlockSpec((D_MODEL, PROJ_CV_TN), lambda i, j: (0, c_cv(j))),
        ],
        out_specs=[
            pl.BlockSpec((PROJ_TM, PROJ_RW_TN), lambda i, j: (i, c_rw(j))),
            pl.BlockSpec((PROJ_TM, PROJ_NA_TN), lambda i, j: (i, c_na(j))),
            pl.BlockSpec((PROJ_TM, PROJ_CV_TN), lambda i, j: (i, c_cv(j))),
        ],
        out_shape=[
            jax.ShapeDtypeStruct((N_TOK, RW_IN), F32),
            jax.ShapeDtypeStruct((N_TOK, NA_IN), F32),
            jax.ShapeDtypeStruct((N_TOK, CV_IN), F32),
        ],
        scratch_shapes=[pltpu.VMEM((PROJ_TM, D_MODEL), BF16)],
        compiler_params=pltpu.CompilerParams(
            dimension_semantics=("arbitrary", "arbitrary"), vmem_limit_bytes=VMEM_LIMIT),
        name="in_proj",
    )(x, mods, g, w_rw, w_na, w_cv)


RW_C = 64
RW_HALO = 8
RW_SB = 16
RW_SB_SQUARINGS = 3
assert RW_C == 4 * RW_SB


def _rwkv_kernel(reverse, final, n_chunks, *refs):
    if final:
        (p_ref, pp_ref, pn_ref, s0_ref, shift_ref, w0_ref, w2_ref, a0_ref, a2_ref, ka_ref, kkw_ref, rk_ref,
         tri_ref, g2_ref, lng_ref, lnb_ref, ysin_ref, bnin_ref, y_ref, sfin_ref, s_scr) = refs
    else:
        (p_ref, pp_ref, pn_ref, s0_ref, shift_ref, w0_ref, w2_ref, a0_ref, a2_ref, ka_ref, kkw_ref, rk_ref,
         tri_ref, ys_ref, bn_ref, sfin_ref, s_scr) = refs
    c = pl.program_id(1)
    cc = (n_chunks - 1 - c) if reverse else c

    @pl.when(c == 0)
    def _():
        s_scr[...] = s0_ref[0]

    x = p_ref[...]
    row = lax.broadcasted_iota(jnp.int32, (RW_C, 1), 0)
    hp = jnp.where(cc > 0, pp_ref[RW_HALO - 1:RW_HALO, :], 0.0)
    hn = jnp.where(cc < n_chunks - 1, pn_ref[0:1, :], 0.0)
    prev = jnp.where(row == 0, hp, pltpu.roll(x, 1, 0))
    nxt = jnp.where(row == RW_C - 1, hn, pltpu.roll(x, RW_C - 1, 0))
    xs = x + shift_ref[0:1, :] * (prev - x) + shift_ref[1:2, :] * (nxt - x)

    r = xs[:, 0:C_RW]
    k = xs[:, C_RW:2 * C_RW]
    v = xs[:, 2 * C_RW:3 * C_RW]
    wd = xs[:, 3 * C_RW:3 * C_RW + 2 * R_W]
    ad = xs[:, 3 * C_RW + 2 * R_W:3 * C_RW + 2 * R_W + 2 * R_A]
    wl = w0_ref[...] + _dot(jnp.tanh(wd).astype(BF16), w2_ref[...])
    softplus_neg = jnp.maximum(-wl, 0.0) + jnp.log(1.0 + jnp.exp(-jnp.abs(wl)))
    lw = -jnp.exp(-softplus_neg - 0.5)
    a_sig = _sigmoid(a0_ref[...] + _dot(ad.astype(BF16), a2_ref[...]))
    kmod = k * (1.0 + (a_sig - 1.0) * ka_ref[...])
    kkf = k * kkw_ref[...]
    bonus_w = r * kmod * rk_ref[...]
    cl = jnp.dot(tri_ref[...], lw, preferred_element_type=F32, precision=lax.Precision.HIGHEST)
    cl_end = cl[0:1, :] if reverse else cl[RW_C - 1:RW_C, :]
    g_in = jnp.exp(cl)
    g_inv = jnp.exp(-cl)
    g_prev = jnp.exp(cl - lw)
    g_end = jnp.exp(cl_end - cl)
    g_tot = jnp.exp(cl_end)

    lane = lax.broadcasted_iota(jnp.int32, (1, LANES), 1)
    m_a = lane < HD
    ri = lax.broadcasted_iota(jnp.int32, (2 * RW_C, 2 * RW_C), 0)
    ci = lax.broadcasted_iota(jnp.int32, (2 * RW_C, 2 * RW_C), 1)
    strict = (ri < ci) if reverse else (ri > ci)
    incl = (ri <= ci) if reverse else (ri >= ci)
    eye = ri == ci
    sub_blk = (ri // RW_SB) == (ci // RW_SB)

    def stack(t):
        return jnp.concatenate([jnp.where(m_a, t, 0.0), jnp.where(m_a, 0.0, t)], 0)

    if final:
        gd = xs[:, 3 * C_RW + 2 * R_W + 2 * R_A:]
        gate = _dot(_sigmoid(gd).astype(BF16), g2_ref[...])

    n = 2 * RW_C
    pairs = range(N_PAIR)
    sls = [slice(p * LANES, (p + 1) * LANES) for p in pairs]
    z_a, z_r, z_v, z_bh, z_kh, bonus = [], [], [], [], [], []
    l_ab, l_ak, t_rb, t_rk = [], [], [], []
    for sl in sls:
        r_p, k_p, v_p, a_p = r[:, sl], kmod[:, sl], v[:, sl], a_sig[:, sl]
        kkf_p = kkf[:, sl]
        kk = kkf_p * lax.rsqrt(jnp.maximum(_head_sum(kkf_p * kkf_p, m_a), 1e-24))
        bb = kk * a_p
        bonus.append(_head_sum(bonus_w[:, sl], m_a) * v_p)
        z_a.append(stack(-kk * g_prev[:, sl]))
        z_r.append(stack(r_p * g_in[:, sl]))
        z_b = stack(bb * g_inv[:, sl])
        z_k = stack(k_p * g_inv[:, sl])
        z_bh.append(stack(bb * g_end[:, sl]))
        z_kh.append(stack(k_p * g_end[:, sl]))
        z_v.append(stack(v_p))
        gram = _dot_t(jnp.concatenate([z_a[-1], z_r[-1]], 0).astype(BF16),
                      jnp.concatenate([z_b, z_k], 0).astype(BF16))
        l_ab.append(jnp.where(strict, gram[:n, :n], 0.0).astype(BF16))
        l_ak.append(jnp.where(strict, gram[:n, n:], 0.0).astype(BF16))
        t_rb.append(jnp.where(incl, gram[n:, :n], 0.0))
        t_rk.append(jnp.where(incl, gram[n:, n:], 0.0))
    xx = [jnp.concatenate([z_a[p], _dot(l_ak[p], z_v[p].astype(BF16))], 1) for p in pairs]
    pw = [jnp.where(sub_blk, l_ab[p], 0.0) for p in pairs]
    n_off = [jnp.where(sub_blk, 0.0, l_ab[p]) for p in pairs]
    td = [jnp.where(eye, 1.0, pw[p].astype(F32)) for p in pairs]
    for _ in range(RW_SB_SQUARINGS):
        pw = [_dot(pw[p], pw[p]).astype(BF16) for p in pairs]
        td = [td[p] + _dot(td[p].astype(BF16), pw[p]) for p in pairs]
    td = [td[p].astype(BF16) for p in pairs]
    mm = [_dot(td[p], n_off[p]).astype(BF16) for p in pairs]
    xx = [_dot(td[p], xx[p].astype(BF16)) for p in pairs]
    xx = [xx[p] + _dot(mm[p], xx[p].astype(BF16)) for p in pairs]
    mm = [_dot(mm[p], mm[p]).astype(BF16) for p in pairs]
    xx = [xx[p] + _dot(mm[p], xx[p].astype(BF16)) for p in pairs]
    res = []
    for p in pairs:
        lhs = jnp.concatenate([jnp.concatenate([t_rb[p], t_rk[p]], 1),
                               jnp.concatenate([z_bh[p].T, z_kh[p].T], 1)], 0).astype(BF16)
        rhs = jnp.concatenate([xx[p], jnp.concatenate([jnp.zeros((n, LANES), F32), z_v[p]], 1)], 0).astype(BF16)
        res.append(_dot(lhs, rhs))
    for p in pairs:
        sl = sls[p]
        q_m = z_r[p] + res[p][:n, :LANES]
        g_m = jnp.where(eye, g_tot[:, sl], 0.0) + res[p][n:, :LANES]
        upd = _dot(jnp.concatenate([q_m, g_m], 0).astype(BF16), s_scr[p].astype(BF16))
        ys = upd[:n] + res[p][:n, LANES:]
        s_scr[p] = upd[n:] + res[p][n:, LANES:]
        y_p = ys[:RW_C] + ys[RW_C:]
        if final:
            y_t = y_p + ysin_ref[:, sl]
            mu = _head_sum(y_t, m_a) * (1.0 / HD)
            d = y_t - mu
            var = _head_sum(d * d, m_a) * (1.0 / HD)
            y_n = d * lax.rsqrt(var + RW_GN_EPS) * lng_ref[:, sl] + lnb_ref[:, sl]
            y_ref[:, sl] = ((y_n + bonus[p] + bnin_ref[:, sl]) * gate[:, sl]).astype(y_ref.dtype)
        else:
            ys_ref[:, sl] = y_p
            bn_ref[:, sl] = bonus[p]

    @pl.when(c == n_chunks - 1)
    def _():
        sfin_ref[0] = s_scr[...]


def _rwkv_pass(reverse, final, batch, seq, base_row, p_rw, s0, wts, extra):
    n_chunks = seq // RW_C
    base_blk = base_row // RW_C
    halo_per_chunk = RW_C // RW_HALO
    n_halo_blk = N_TOK // RW_HALO

    def chunk(c):
        return (n_chunks - 1 - c) if reverse else c

    def row_blk(b, c):
        return base_blk + b * n_chunks + chunk(c)

    def prev_blk(b, c):
        return jnp.maximum(row_blk(b, c) * halo_per_chunk - 1, 0)

    def next_blk(b, c):
        return jnp.minimum((row_blk(b, c) + 1) * halo_per_chunk, n_halo_blk - 1)

    full = lambda shape: pl.BlockSpec(shape, lambda b, c: (0,) * len(shape))
    tok = lambda width: pl.BlockSpec((RW_C, width), lambda b, c: (row_blk(b, c), 0))
    loc = lambda width: pl.BlockSpec((RW_C, width), lambda b, c: (b * n_chunks + chunk(c), 0))
    state = pl.BlockSpec((1, N_PAIR, LANES, LANES), lambda b, c: (b, 0, 0, 0))
    in_specs = [
        tok(RW_IN),
        pl.BlockSpec((RW_HALO, RW_IN), lambda b, c: (prev_blk(b, c), 0)),
        pl.BlockSpec((RW_HALO, RW_IN), lambda b, c: (next_blk(b, c), 0)),
        state,
        full((2, RW_IN)), full((1, C_RW)), full((2 * R_W, C_RW)), full((1, C_RW)), full((2 * R_A, C_RW)),
        full((1, C_RW)), full((1, C_RW)), full((1, C_RW)), full((RW_C, RW_C)),
    ]
    args = [p_rw, p_rw, p_rw, s0] + list(wts)
    if final:
        in_specs += [full((R_G, C_RW)), full((1, C_RW)), full((1, C_RW)), loc(C_RW), loc(C_RW)]
        args += list(extra)
        out_specs = [loc(C_RW), state]
        out_shape = [jax.ShapeDtypeStruct((batch * seq, C_RW), BF16),
                     jax.ShapeDtypeStruct((batch, N_PAIR, LANES, LANES), F32)]
    else:
        out_specs = [loc(C_RW), loc(C_RW), state]
        out_shape = [jax.ShapeDtypeStruct((batch * seq, C_RW), F32), jax.ShapeDtypeStruct((batch * seq, C_RW), F32),
                     jax.ShapeDtypeStruct((batch, N_PAIR, LANES, LANES), F32)]
    return pl.pallas_call(
        functools.partial(_rwkv_kernel, reverse, final, n_chunks),
        grid=(batch, n_chunks),
        in_specs=in_specs,
        out_specs=out_specs,
        out_shape=out_shape,
        scratch_shapes=[pltpu.VMEM((N_PAIR, LANES, LANES), F32)],
        compiler_params=pltpu.CompilerParams(
            dimension_semantics=("arbitrary", "arbitrary"), vmem_limit_bytes=VMEM_LIMIT),
        name="rwkv_bwd" if reverse else "rwkv_fwd",
    )(*args)


def _state_to_tiles(s):
    b = s.shape[0]
    st = jnp.swapaxes(s, -1, -2).reshape(b, N_PAIR, 2, HD, HD)
    z = jnp.zeros((b, N_PAIR, HD, HD), s.dtype)
    top = jnp.concatenate([st[:, :, 0], z], -1)
    bot = jnp.concatenate([z, st[:, :, 1]], -1)
    return jnp.concatenate([top, bot], -2)


def _tiles_to_state(t):
    b = t.shape[0]
    s_a = t[:, :, :HD, :HD]
    s_b = t[:, :, HD:, HD:]
    st = jnp.stack([s_a, s_b], 2).reshape(b, H_RW, HD, HD)
    return jnp.swapaxes(st, -1, -2)


def _rwkv_mix(p_rw, s0_fwd, s0_bwd, batch, seq, base_row, W):
    def dir_weights(d):
        pad = lambda m: jnp.zeros((2 * m.shape[1], C_RW), BF16).at[d * m.shape[1]:(d + 1) * m.shape[1]].set(
            m[d].astype(BF16))
        tri = np.triu(np.ones((RW_C, RW_C), np.float32)) if d == 1 else np.tril(np.ones((RW_C, RW_C), np.float32))
        return [W['rw_shift'], W['rw_w0'][d][None], pad(W['rw_w2']), W['rw_a0'][d][None], pad(W['rw_a2']),
                W['rw_ka'][None], W['rw_kk'][None], W['rw_rk'].reshape(1, C_RW), jnp.asarray(tri)]

    ys_b, bn_b, s_b = _rwkv_pass(True, False, batch, seq, base_row, p_rw, s0_bwd, dir_weights(1), None)
    extra = [W['rw_g2'].astype(BF16), W['rw_ln_g'][None], W['rw_ln_b'][None], ys_b, bn_b]
    y, s_f = _rwkv_pass(False, True, batch, seq, base_row, p_rw, s0_fwd, dir_weights(0), extra)
    return y, s_f, s_b


ATT_TQ = 256


def _attn_kernel(n_q, has_ctx, *refs):
    if has_ctx:
        q_ref, k_ref, v_ref, kc_ref, vc_ref, bias_ref, o_ref = refs
    else:
        q_ref, k_ref, v_ref, o_ref = refs
    lane = lax.broadcasted_iota(jnp.int32, (1, LANES), 1)
    m_a = lane < HD
    k = k_ref[...].astype(BF16)
    v = v_ref[...].astype(BF16)
    if has_ctx:
        kc = kc_ref[0].astype(BF16)
        vc = vc_ref[0].astype(BF16)
    for t in range(n_q // ATT_TQ):
        rows = slice(t * ATT_TQ, (t + 1) * ATT_TQ)
        q = q_ref[rows, :] * (HD ** -0.5)
        outs = []
        for h in range(2):
            q_h = jnp.where(m_a if h == 0 else jnp.logical_not(m_a), q, 0.0).astype(BF16)
            s = _dot_t(q_h, k)
            if has_ctx:
                s = s + bias_ref[h, rows, :]
                s_c = _dot_t(q_h, kc)
                m = jnp.maximum(jnp.max(s, -1, keepdims=True), jnp.max(s_c, -1, keepdims=True))
                e_c = jnp.exp(s_c - m)
                e = jnp.exp(s - m)
                den = jnp.sum(e, -1, keepdims=True) + jnp.sum(e_c, -1, keepdims=True)
                o = _dot(e.astype(BF16), v) + _dot(e_c.astype(BF16), vc)
            else:
                m = jnp.max(s, -1, keepdims=True)
                e = jnp.exp(s - m)
                den = jnp.sum(e, -1, keepdims=True)
                o = _dot(e.astype(BF16), v)
            outs.append(o / den)
        o_ref[rows, :] = jnp.where(m_a, outs[0], outs[1]).astype(o_ref.dtype)


def _attention(p_na, batch, seq, base_row, kc=None, vc=None, bias=None):
    has_ctx = kc is not None
    base_blk = base_row // seq
    tok = lambda off: pl.BlockSpec((seq, LANES), lambda p, b: (base_blk + b, off + p))
    in_specs = [tok(0), tok(N_PAIR), tok(2 * N_PAIR)]
    args = [p_na, p_na, p_na]
    if has_ctx:
        cache = pl.BlockSpec((1, PAST_LEN, LANES), lambda p, b: (b, 0, p))
        in_specs += [cache, cache, pl.BlockSpec((2, seq, seq), lambda p, b: (p, 0, 0))]
        args += [kc, vc, bias]
    return pl.pallas_call(
        functools.partial(_attn_kernel, seq, has_ctx),
        grid=(N_PAIR, batch),
        in_specs=in_specs,
        out_specs=pl.BlockSpec((seq, LANES), lambda p, b: (b, p)),
        out_shape=jax.ShapeDtypeStruct((batch * seq, C_NA), BF16),
        compiler_params=pltpu.CompilerParams(
            dimension_semantics=("arbitrary", "arbitrary"), vmem_limit_bytes=VMEM_LIMIT),
        name="natten_lat" if has_ctx else "attn_ctx",
    )(*args)


def _natten_bias(rpb):
    rows = DEC_SEQ // GRID_W
    kh = min(NA_KH_MAX, rows)
    qi = np.arange(rows)
    si = np.clip(qi - kh // 2, 0, rows - kh)
    ok_r = (qi[None, :] >= si[:, None]) & (qi[None, :] < si[:, None] + kh)
    idx_r = np.clip(qi[None, :] - qi[:, None] + NA_KH_MAX - 1, 0, 2 * NA_KH_MAX - 2)
    qj = np.arange(GRID_W)
    wj = np.clip(qj - NA_KW // 2, 0, GRID_W - NA_KW)
    ok_c = (qj[None, :] >= wj[:, None]) & (qj[None, :] < wj[:, None] + NA_KW)
    idx_c = np.clip(qj[None, :] - qj[:, None] + NA_KW - 1, 0, 2 * NA_KW - 2)
    ok = ok_r[:, None, :, None] & ok_c[None, :, None, :]
    b = rpb[:, idx_r[:, None, :, None], idx_c[None, :, None, :]]
    b = jnp.where(jnp.asarray(ok)[None], b, NEG)
    return b.reshape(H_NA, DEC_SEQ, DEC_SEQ)


CV_PAD = 16
CV_TT = 256


def _conv_kernel(seq, val_ref, gate_ref, dw_ref, dwb_ref, lng_ref, lnb_ref, o_ref, u_scr, c_scr):
    zeros = jnp.zeros((CV_PAD, C_CV), F32)
    u_scr[0:CV_PAD, :] = zeros
    u_scr[seq + CV_PAD:seq + 2 * CV_PAD, :] = zeros
    u_scr[CV_PAD:seq + CV_PAD, :] = val_ref[...] * _sigmoid(gate_ref[...])
    off = CV_PAD - CONV_K // 2
    for cb in range(C_CV // LANES):
        cols = slice(cb * LANES, (cb + 1) * LANES)
        for t in range(seq // CV_TT):
            acc = jnp.zeros((CV_TT, LANES), F32)
            for j in range(CONV_K):
                acc = acc + dw_ref[j:j + 1, cols] * u_scr[t * CV_TT + off + j:t * CV_TT + off + j + CV_TT, cols]
            c_scr[t * CV_TT:(t + 1) * CV_TT, cols] = acc
    u = c_scr[...] + dwb_ref[...]
    mu = jnp.mean(u, -1, keepdims=True)
    d = u - mu
    var = jnp.mean(d * d, -1, keepdims=True)
    y = d * lax.rsqrt(var + LN_EPS) * lng_ref[...] + lnb_ref[...]
    o_ref[...] = (y * _sigmoid(y)).astype(o_ref.dtype)


def _conv_module(p_cv, batch, seq, base_row, W):
    base_blk = base_row // seq
    full = lambda shape: pl.BlockSpec(shape, lambda b: (0,) * len(shape))
    return pl.pallas_call(
        functools.partial(_conv_kernel, seq),
        grid=(batch,),
        in_specs=[
            pl.BlockSpec((seq, C_CV), lambda b: (base_blk + b, 0)),
            pl.BlockSpec((seq, C_CV), lambda b: (base_blk + b, 1)),
            full((CONV_K, C_CV)), full((1, C_CV)), full((1, C_CV)), full((1, C_CV)),
        ],
        out_specs=pl.BlockSpec((seq, C_CV), lambda b: (b, 0)),
        out_shape=jax.ShapeDtypeStruct((batch * seq, C_CV), BF16),
        scratch_shapes=[pltpu.VMEM((seq + 2 * CV_PAD, C_CV), F32), pltpu.VMEM((seq, C_CV), F32)],
        compiler_params=pltpu.CompilerParams(
            dimension_semantics=("arbitrary",), vmem_limit_bytes=VMEM_LIMIT),
        name="conv_module",
    )(p_cv, p_cv, W['cv_dw'], W['cv_dw_b'][None], W['cv_ln_g'][None], W['cv_ln_b'][None])


OUT_TM = 256


def _out_kernel(yrw_ref, yna_ref, ycv_ref, x_ref, mods_ref, g_ref, w_ref, rt_ref, xo_ref, h_ref, lg_ref):
    y = jnp.concatenate([yrw_ref[...], yna_ref[...], ycv_ref[...]], -1)
    x_new = x_ref[...] + mods_ref[0, 2:3, :] * _dot(y, w_ref[...])
    xo_ref[...] = x_new
    n = x_new * lax.rsqrt(jnp.mean(x_new * x_new, -1, keepdims=True) + NORM_EPS)
    h = (n * g_ref[...]) * (1.0 + mods_ref[0, 4:5, :]) + mods_ref[0, 3:4, :]
    h_ref[...] = h.astype(BF16)
    lg_ref[...] = jnp.dot(h, rt_ref[...], preferred_element_type=F32, precision=lax.Precision.HIGHEST)


def _out_project(y_rw, y_na, y_cv, x, mods, g, w_bf16, rt):
    per_seg = SEG // OUT_TM
    tok = lambda width: pl.BlockSpec((OUT_TM, width), lambda i: (i, 0))
    return pl.pallas_call(
        _out_kernel,
        grid=(N_TOK // OUT_TM,),
        in_specs=[
            tok(C_RW), tok(C_NA), tok(C_CV), tok(D_MODEL),
            pl.BlockSpec((1, 6, D_MODEL), lambda i: (i // per_seg, 0, 0)),
            pl.BlockSpec((1, D_MODEL), lambda i: (0, 0)),
            pl.BlockSpec((D_MODEL, D_MODEL), lambda i: (0, 0)),
            pl.BlockSpec((D_MODEL, RT_PAD), lambda i: (0, 0)),
        ],
        out_specs=[tok(D_MODEL), tok(D_MODEL), tok(RT_PAD)],
        out_shape=[
            jax.ShapeDtypeStruct((N_TOK, D_MODEL), F32),
            jax.ShapeDtypeStruct((N_TOK, D_MODEL), BF16),
            jax.ShapeDtypeStruct((N_TOK, RT_PAD), F32),
        ],
        compiler_params=pltpu.CompilerParams(
            dimension_semantics=("arbitrary",), vmem_limit_bytes=VMEM_LIMIT),
        name="out_proj",
    )(y_rw, y_na, y_cv, x, mods, g, w_bf16, rt)


def _moe_kernel(blk_e_ref, n_used_ref, xb_ref, rw_ref, w13_ref, w2_ref, o_ref):
    i = pl.program_id(0)

    @pl.when(i < n_used_ref[0])
    def _():
        gu = _dot(xb_ref[...], w13_ref[0])
        gt = gu[:, :D_EXPERT]
        up = gu[:, D_EXPERT:]
        a = (gt * _sigmoid(gt) * up).astype(BF16)
        o_ref[...] = _dot(a, w2_ref[0]) * rw_ref[...]

    @pl.when(i >= n_used_ref[0])
    def _():
        o_ref[...] = jnp.zeros_like(o_ref)


def _moe_blocks(blk_e, n_used, xb, row_w, w13_bf16, w2_bf16):
    grid_spec = pltpu.PrefetchScalarGridSpec(
        num_scalar_prefetch=2,
        grid=(MOE_NBLK,),
        in_specs=[
            pl.BlockSpec((MOE_BM, D_MODEL), lambda i, be, nu: (i, 0)),
            pl.BlockSpec((MOE_BM, 1), lambda i, be, nu: (i, 0)),
            pl.BlockSpec((1, D_MODEL, 2 * D_EXPERT), lambda i, be, nu: (be[i], 0, 0)),
            pl.BlockSpec((1, D_EXPERT, D_MODEL), lambda i, be, nu: (be[i], 0, 0)),
        ],
        out_specs=pl.BlockSpec((MOE_BM, D_MODEL), lambda i, be, nu: (i, 0)),
    )
    return pl.pallas_call(
        _moe_kernel,
        grid_spec=grid_spec,
        out_shape=jax.ShapeDtypeStruct((MOE_NBLK * MOE_BM, D_MODEL), F32),
        compiler_params=pltpu.CompilerParams(
            dimension_semantics=("arbitrary",), vmem_limit_bytes=VMEM_LIMIT),
        name="moe_blocks",
    )(blk_e, n_used, xb, row_w, w13_bf16, w2_bf16)


def _hmoe(h_bf16, logits, w13_bf16, w2_bf16):
    n = N_TOK
    lg = logits[:, :N_GROUPS]
    grp = jnp.argmax(lg, -1).astype(jnp.int32)
    gate_g = jnp.take_along_axis(jax.nn.softmax(lg, -1), grp[:, None], -1)
    le = logits[:, N_GROUPS:N_GROUPS + N_EXPERTS].reshape(n, N_GROUPS, EXP_PER_GROUP)
    le = jnp.take_along_axis(le, grp[:, None, None], 1)[:, 0]
    top_l, top_i = lax.top_k(le, TOP_K)
    w = gate_g * jax.nn.softmax(top_l, -1)
    eid = grp[:, None] * EXP_PER_GROUP + top_i.astype(jnp.int32)

    a_tot = n * TOP_K
    rows = MOE_NBLK * MOE_BM
    flat_e = eid.reshape(-1)
    flat_t = jnp.arange(a_tot, dtype=jnp.int32) // TOP_K
    flat_w = w.reshape(-1)
    onehot = (flat_e[:, None] == jnp.arange(N_EXPERTS, dtype=jnp.int32)[None]).astype(jnp.int32)
    csum = jnp.cumsum(onehot, axis=0)
    counts = csum[-1]
    rank = jnp.take_along_axis(csum, flat_e[:, None], 1)[:, 0] - 1
    pcounts = (counts + MOE_BM - 1) // MOE_BM * MOE_BM
    pends = jnp.cumsum(pcounts)
    pstarts = pends - pcounts
    dest = pstarts[flat_e] + rank
    row_tok = jnp.zeros((rows,), jnp.int32).at[dest].set(flat_t)
    row_w = jnp.zeros((rows,), F32).at[dest].set(flat_w)
    blk_start = jnp.arange(MOE_NBLK, dtype=jnp.int32) * MOE_BM
    n_used = (pends[-1] // MOE_BM).astype(jnp.int32)
    blk_e = jnp.searchsorted(pends, blk_start, side='right').astype(jnp.int32)
    last_e = jnp.searchsorted(pends, pends[-1] - 1, side='right').astype(jnp.int32)
    blk_e = jnp.minimum(blk_e, last_e)
    xb = h_bf16[row_tok]
    yb = _moe_blocks(blk_e, n_used.reshape(1), xb, row_w[:, None], w13_bf16, w2_bf16)
    d2 = dest.reshape(n, TOP_K)
    return yb[d2[:, 0]] + yb[d2[:, 1]]


def _rmsnorm_final(x, g):
    return x * lax.rsqrt(jnp.mean(x * x, -1, keepdims=True) + NORM_EPS) * g


def kernel(x_prompt, x_sample, cache_k, cache_v, state_rwkv, c, c_ctx, norm1_g, w_mod, b_mod, w_in,
           rw_shift, rw_w0, rw_w2, rw_a0, rw_a2, rw_g2, rw_kk, rw_ka, rw_rk, rw_ln_g, rw_ln_b, na_rpb,
           cv_dw, cv_dw_b, cv_ln_g, cv_ln_b, w_out, norm2_g, rt_group, rt_expert, ex_w13, ex_w2, final_g):
    x = jnp.concatenate([x_prompt.reshape(N_CTX_TOK, D_MODEL), x_sample.reshape(N_LAT_TOK, D_MODEL)], 0)
    cond = jnp.concatenate([c_ctx[None], c], 0)
    seg_cond = np.concatenate([np.zeros(N_CTX_SEG, np.int32), 1 + np.arange(DEC_BATCH, dtype=np.int32)])
    zero_state = jnp.zeros((BATCH, N_PAIR, LANES, LANES), F32)
    ks_out, vs_out, ss_out = [], [], []
    for l in range(DEPTH):
        W = dict(rw_shift=rw_shift[l], rw_w0=rw_w0[l], rw_w2=rw_w2[l], rw_a0=rw_a0[l], rw_a2=rw_a2[l],
                 rw_g2=rw_g2[l], rw_kk=rw_kk[l], rw_ka=rw_ka[l], rw_rk=rw_rk[l], rw_ln_g=rw_ln_g[l],
                 rw_ln_b=rw_ln_b[l], cv_dw=cv_dw[l], cv_dw_b=cv_dw_b[l], cv_ln_g=cv_ln_g[l],
                 cv_ln_b=cv_ln_b[l])
        m = jax.nn.silu(cond) @ w_mod[l] + b_mod[l]
        mods = m.reshape(1 + DEC_BATCH, 6, D_MODEL)[seg_cond]
        w_l = w_in[l].astype(BF16)
        p_rw, p_na, p_cv = _project(x, mods, norm1_g[l][None], w_l[:, :RW_IN], w_l[:, RW_IN:RW_IN + NA_IN],
                                    w_l[:, RW_IN + NA_IN:])
        y_rw_c, s_f, s_b = _rwkv_mix(p_rw, zero_state, zero_state, BATCH, SEQ, 0, W)
        y_na_c = _attention(p_na, BATCH, SEQ, 0)
        y_cv_c = _conv_module(p_cv, BATCH, SEQ, 0, W)
        s0 = state_rwkv[:, l]
        y_rw_l, _, _ = _rwkv_mix(p_rw, _state_to_tiles(s0[:, 0]), _state_to_tiles(s0[:, 1]),
                                 DEC_BATCH, DEC_SEQ, N_CTX_TOK, W)
        y_na_l = _attention(p_na, DEC_BATCH, DEC_SEQ, N_CTX_TOK,
                            cache_k[:, l].reshape(DEC_BATCH, PAST_LEN, C_NA),
                            cache_v[:, l].reshape(DEC_BATCH, PAST_LEN, C_NA), _natten_bias(na_rpb[l]))
        y_cv_l = _conv_module(p_cv, DEC_BATCH, DEC_SEQ, N_CTX_TOK, W)
        y_rw = jnp.concatenate([y_rw_c, y_rw_l], 0)
        y_na = jnp.concatenate([y_na_c, y_na_l], 0)
        y_cv = jnp.concatenate([y_cv_c, y_cv_l], 0)
        ks_out.append(p_na[:N_CTX_TOK, C_NA:2 * C_NA].reshape(BATCH, SEQ, H_NA, HD))
        vs_out.append(p_na[:N_CTX_TOK, 2 * C_NA:].reshape(BATCH, SEQ, H_NA, HD))
        ss_out.append(jnp.stack([_tiles_to_state(s_f), _tiles_to_state(s_b)], 1))
        rt = jnp.concatenate([rt_group[l], rt_expert[l],
                              jnp.zeros((D_MODEL, RT_PAD - N_GROUPS - N_EXPERTS), F32)], 1)
        x_new, h2, logits = _out_project(y_rw, y_na, y_cv, x, mods, norm2_g[l][None], w_out[l].astype(BF16), rt)
        moe = _hmoe(h2, logits, ex_w13[l].astype(BF16), ex_w2[l].astype(BF16))
        x = x_new + mods[:, 5].repeat(SEG, axis=0) * moe
    y = _rmsnorm_final(x, final_g)
    y_prompt = y[:N_CTX_TOK].reshape(BATCH, SEQ, D_MODEL)
    y_sample = y[N_CTX_TOK:].reshape(DEC_BATCH, DEC_SEQ, D_MODEL)
    return (y_prompt, y_sample, jnp.stack(ks_out, axis=1), jnp.stack(vs_out, axis=1),
            jnp.stack(ss_out, axis=1))
```

```python
import functools

import jax
import jax.numpy as jnp
import numpy as np
from jax import lax
from jax.experimental import pallas as pl
from jax.experimental.pallas import tpu as pltpu

D_MODEL = 2048
BATCH = 16
SEQ = 256
DEPTH = 4
DEC_BATCH = 8
DEC_SEQ = 1024
PAST_LEN = 512

GRID_W = 64
HD = 64
H_RW = 12
C_RW = H_RW * HD
H_NA = 12
C_NA = H_NA * HD
C_CV = D_MODEL - C_RW - C_NA
R_W = 64
R_A = 64
R_G = 128
RW_IN = 3 * C_RW + 2 * R_W + 2 * R_A + R_G
NA_IN = 3 * C_NA
CV_IN = 2 * C_CV
NA_KH_MAX = 8
NA_KW = 16
CONV_K = 31
N_GROUPS = 4
EXP_PER_GROUP = 8
N_EXPERTS = N_GROUPS * EXP_PER_GROUP
TOP_K = 2
D_EXPERT = 1024
NORM_EPS = 1e-6
LN_EPS = 1e-5
RW_GN_EPS = 64e-5
NEG = -1e30

LANES = 128
N_PAIR = H_RW // 2
SEG = 1024
N_CTX_TOK = BATCH * SEQ
N_LAT_TOK = DEC_BATCH * DEC_SEQ
N_TOK = N_CTX_TOK + N_LAT_TOK
N_SEG = N_TOK // SEG
N_CTX_SEG = N_CTX_TOK // SEG
RT_PAD = LANES
MOE_BM = 256
MOE_NBLK = (N_TOK * TOP_K) // MOE_BM + N_EXPERTS
VMEM_LIMIT = 48 * 1024 * 1024

BF16 = jnp.bfloat16
F32 = jnp.float32


def _dot(a, b):
    return jnp.dot(a, b, preferred_element_type=F32)


def _dot_t(a, b):
    return lax.dot_general(a, b, (((1,), (1,)), ((), ())), preferred_element_type=F32)


def _sigmoid(x):
    return 1.0 / (1.0 + jnp.exp(-x))


def _head_sum(x, m_a):
    s_a = jnp.sum(jnp.where(m_a, x, 0.0), -1, keepdims=True)
    s_b = jnp.sum(jnp.where(m_a, 0.0, x), -1, keepdims=True)
    return jnp.where(m_a, s_a, s_b)


NORM_TM = 256


def _modulated_norm(x, g, shift, scale):
    y = x * lax.rsqrt(jnp.mean(x * x, -1, keepdims=True) + NORM_EPS)
    return (y * g) * (1.0 + scale) + shift


def _norm_kernel(x_ref, mods_ref, g_ref, h_ref):
    h_ref[...] = _modulated_norm(x_ref[...], g_ref[...], mods_ref[0, 0:1, :], mods_ref[0, 1:2, :]).astype(BF16)


def _norm_mod(x, mods, g):
    tok = pl.BlockSpec((NORM_TM, D_MODEL), lambda i: (i, 0))
    return pl.pallas_call(
        _norm_kernel,
        grid=(N_TOK // NORM_TM,),
        in_specs=[tok, pl.BlockSpec((1, 6, D_MODEL), lambda i: (i // (SEG // NORM_TM), 0, 0)),
                  pl.BlockSpec((1, D_MODEL), lambda i: (0, 0))],
        out_specs=tok,
        out_shape=jax.ShapeDtypeStruct((N_TOK, D_MODEL), BF16),
        compiler_params=pltpu.CompilerParams(
            dimension_semantics=("arbitrary",), vmem_limit_bytes=VMEM_LIMIT),
        name="norm_mod",
    )(x, mods, g)


PROJ_TM = SEG
PROJ_RW_TN = RW_IN // 3
PROJ_NA_TN = NA_IN // 3
PROJ_CV_TN = CV_IN // 2
PROJ_STEPS = 8


def _proj_kernel(h_ref, wrw_ref, wna_ref, wcv_ref, orw_ref, ona_ref, ocv_ref):
    j = pl.program_id(1)

    @pl.when(j < 3)
    def _():
        orw_ref[...] = _dot(h_ref[...], wrw_ref[0])

    @pl.when((j >= 3) & (j < 6))
    def _():
        ona_ref[...] = _dot(h_ref[...], wna_ref[0])

    @pl.when(j >= 6)
    def _():
        ocv_ref[...] = _dot(h_ref[...], wcv_ref[0])


def _project(h, layer, w_rw, w_na, w_cv):
    def c_rw(j):
        return jnp.minimum(j, 2)

    def c_na(j):
        return jnp.clip(j - 3, 0, 2)

    def c_cv(j):
        return jnp.clip(j - 6, 0, 1)

    return pl.pallas_call(
        _proj_kernel,
        grid=(N_TOK // PROJ_TM, PROJ_STEPS),
        in_specs=[
            pl.BlockSpec((PROJ_TM, D_MODEL), lambda i, j: (i, 0)),
            pl.BlockSpec((1, D_MODEL, PROJ_RW_TN), lambda i, j: (layer, 0, c_rw(j))),
            pl.BlockSpec((1, D_MODEL, PROJ_NA_TN), lambda i, j: (layer, 0, c_na(j))),
            pl.BlockSpec((1, D_MODEL, PROJ_CV_TN), lambda i, j: (layer, 0, c_cv(j))),
        ],
        out_specs=[
            pl.BlockSpec((PROJ_TM, PROJ_RW_TN), lambda i, j: (i, c_rw(j))),
            pl.BlockSpec((PROJ_TM, PROJ_NA_TN), lambda i, j: (i, c_na(j))),
            pl.BlockSpec((PROJ_TM, PROJ_CV_TN), lambda i, j: (i, c_cv(j))),
        ],
        out_shape=[
            jax.ShapeDtypeStruct((N_TOK, RW_IN), F32),
            jax.ShapeDtypeStruct((N_TOK, NA_IN), F32),
            jax.ShapeDtypeStruct((N_TOK, CV_IN), F32),
        ],
        compiler_params=pltpu.CompilerParams(
            dimension_semantics=("arbitrary", "arbitrary"), vmem_limit_bytes=VMEM_LIMIT),
        name="in_proj",
    )(h, w_rw, w_na, w_cv)


RW_C = 64
RW_HALO = 8
RW_SB = 16
RW_SB_SQUARINGS = 3
assert RW_C == 4 * RW_SB


def _rwkv_kernel(reverse, final, n_chunks, *refs):
    if final:
        (p_ref, pp_ref, pn_ref, s0_ref, shift_ref, w0_ref, w2_ref, a0_ref, a2_ref, ka_ref, kkw_ref, rk_ref,
         tri_ref, g2_ref, lng_ref, lnb_ref, ysin_ref, bnin_ref, y_ref, sfin_ref, s_scr) = refs
    else:
        (p_ref, pp_ref, pn_ref, s0_ref, shift_ref, w0_ref, w2_ref, a0_ref, a2_ref, ka_ref, kkw_ref, rk_ref,
         tri_ref, ys_ref, bn_ref, sfin_ref, s_scr) = refs
    c = pl.program_id(1)
    cc = (n_chunks - 1 - c) if reverse else c

    @pl.when(c == 0)
    def _():
        s_scr[...] = s0_ref[0]

    x = p_ref[...]
    row = lax.broadcasted_iota(jnp.int32, (RW_C, 1), 0)
    hp = jnp.where(cc > 0, pp_ref[RW_HALO - 1:RW_HALO, :], 0.0)
    hn = jnp.where(cc < n_chunks - 1, pn_ref[0:1, :], 0.0)
    prev = jnp.where(row == 0, hp, pltpu.roll(x, 1, 0))
    nxt = jnp.where(row == RW_C - 1, hn, pltpu.roll(x, RW_C - 1, 0))
    xs = x + shift_ref[0:1, :] * (prev - x) + shift_ref[1:2, :] * (nxt - x)

    r = xs[:, 0:C_RW]
    k = xs[:, C_RW:2 * C_RW]
    v = xs[:, 2 * C_RW:3 * C_RW]
    wd = xs[:, 3 * C_RW:3 * C_RW + 2 * R_W]
    ad = xs[:, 3 * C_RW + 2 * R_W:3 * C_RW + 2 * R_W + 2 * R_A]
    wl = w0_ref[...] + _dot(jnp.tanh(wd).astype(BF16), w2_ref[...])
    softplus_neg = jnp.maximum(-wl, 0.0) + jnp.log(1.0 + jnp.exp(-jnp.abs(wl)))
    lw = -jnp.exp(-softplus_neg - 0.5)
    a_sig = _sigmoid(a0_ref[...] + _dot(ad.astype(BF16), a2_ref[...]))
    kmod = k * (1.0 + (a_sig - 1.0) * ka_ref[...])
    kkf = k * kkw_ref[...]
    bonus_w = r * kmod * rk_ref[...]
    cl = jnp.dot(tri_ref[...], lw, preferred_element_type=F32, precision=lax.Precision.HIGHEST)
    cl_end = cl[0:1, :] if reverse else cl[RW_C - 1:RW_C, :]
    g_in = jnp.exp(cl)
    g_inv = jnp.exp(-cl)
    g_prev = jnp.exp(cl - lw)
    g_end = jnp.exp(cl_end - cl)
    g_tot = jnp.exp(cl_end)

    lane = lax.broadcasted_iota(jnp.int32, (1, LANES), 1)
    m_a = lane < HD
    ri = lax.broadcasted_iota(jnp.int32, (2 * RW_C, 2 * RW_C), 0)
    ci = lax.broadcasted_iota(jnp.int32, (2 * RW_C, 2 * RW_C), 1)
    strict = (ri < ci) if reverse else (ri > ci)
    incl = (ri <= ci) if reverse else (ri >= ci)
    eye = ri == ci
    sub_blk = (ri // RW_SB) == (ci // RW_SB)

    def stack(t):
        return jnp.concatenate([jnp.where(m_a, t, 0.0), jnp.where(m_a, 0.0, t)], 0)

    if final:
        gd = xs[:, 3 * C_RW + 2 * R_W + 2 * R_A:]
        gate = _dot(_sigmoid(gd).astype(BF16), g2_ref[...])

    n = 2 * RW_C
    pairs = range(N_PAIR)
    sls = [slice(p * LANES, (p + 1) * LANES) for p in pairs]
    z_a, z_r, z_v, z_bh, z_kh, bonus = [], [], [], [], [], []
    l_ab, l_ak, t_rb, t_rk = [], [], [], []
    for sl in sls:
        r_p, k_p, v_p, a_p = r[:, sl], kmod[:, sl], v[:, sl], a_sig[:, sl]
        kkf_p = kkf[:, sl]
        kk = kkf_p * lax.rsqrt(jnp.maximum(_head_sum(kkf_p * kkf_p, m_a), 1e-24))
        bb = kk * a_p
        bonus.append(_head_sum(bonus_w[:, sl], m_a) * v_p)
        z_a.append(stack(-kk * g_prev[:, sl]))
        z_r.append(stack(r_p * g_in[:, sl]))
        z_b = stack(bb * g_inv[:, sl])
        z_k = stack(k_p * g_inv[:, sl])
        z_bh.append(stack(bb * g_end[:, sl]))
        z_kh.append(stack(k_p * g_end[:, sl]))
        z_v.append(stack(v_p))
        gram = _dot_t(jnp.concatenate([z_a[-1], z_r[-1]], 0).astype(BF16),
                      jnp.concatenate([z_b, z_k], 0).astype(BF16))
        l_ab.append(jnp.where(strict, gram[:n, :n], 0.0).astype(BF16))
        l_ak.append(jnp.where(strict, gram[:n, n:], 0.0).astype(BF16))
        t_rb.append(jnp.where(incl, gram[n:, :n], 0.0))
        t_rk.append(jnp.where(incl, gram[n:, n:], 0.0))
    xx = [jnp.concatenate([z_a[p], _dot(l_ak[p], z_v[p].astype(BF16))], 1) for p in pairs]
    pw = [jnp.where(sub_blk, l_ab[p], 0.0) for p in pairs]
    n_off = [jnp.where(sub_blk, 0.0, l_ab[p]) for p in pairs]
    td = [jnp.where(eye, 1.0, pw[p].astype(F32)) for p in pairs]
    for _ in range(RW_SB_SQUARINGS):
        pw = [_dot(pw[p], pw[p]).astype(BF16) for p in pairs]
        td = [td[p] + _dot(td[p].astype(BF16), pw[p]) for p in pairs]
    td = [td[p].astype(BF16) for p in pairs]
    mm = [_dot(td[p], n_off[p]).astype(BF16) for p in pairs]
    xx = [_dot(td[p], xx[p].astype(BF16)) for p in pairs]
    xx = [xx[p] + _dot(mm[p], xx[p].astype(BF16)) for p in pairs]
    mm = [_dot(mm[p], mm[p]).astype(BF16) for p in pairs]
    xx = [xx[p] + _dot(mm[p], xx[p].astype(BF16)) for p in pairs]
    res = []
    for p in pairs:
        lhs = jnp.concatenate([jnp.concatenate([t_rb[p], t_rk[p]], 1),
                               jnp.concatenate([z_bh[p].T, z_kh[p].T], 1)], 0).astype(BF16)
        rhs = jnp.concatenate([xx[p], jnp.concatenate([jnp.zeros((n, LANES), F32), z_v[p]], 1)], 0).astype(BF16)
        res.append(_dot(lhs, rhs))
    for p in pairs:
        sl = sls[p]
        q_m = z_r[p] + res[p][:n, :LANES]
        g_m = jnp.where(eye, g_tot[:, sl], 0.0) + res[p][n:, :LANES]
        upd = _dot(jnp.concatenate([q_m, g_m], 0).astype(BF16), s_scr[p].astype(BF16))
        ys = upd[:n] + res[p][:n, LANES:]
        s_scr[p] = upd[n:] + res[p][n:, LANES:]
        y_p = ys[:RW_C] + ys[RW_C:]
        if final:
            y_t = y_p + ysin_ref[:, sl]
            mu = _head_sum(y_t, m_a) * (1.0 / HD)
            d = y_t - mu
            var = _head_sum(d * d, m_a) * (1.0 / HD)
            y_n = d * lax.rsqrt(var + RW_GN_EPS) * lng_ref[:, sl] + lnb_ref[:, sl]
            y_ref[:, sl] = ((y_n + bonus[p] + bnin_ref[:, sl]) * gate[:, sl]).astype(y_ref.dtype)
        else:
            ys_ref[:, sl] = y_p
            bn_ref[:, sl] = bonus[p]

    @pl.when(c == n_chunks - 1)
    def _():
        sfin_ref[0] = s_scr[...]


def _rwkv_pass(reverse, final, batch, seq, base_row, p_rw, s0, wts, extra):
    n_chunks = seq // RW_C
    base_blk = base_row // RW_C
    halo_per_chunk = RW_C // RW_HALO
    n_halo_blk = N_TOK // RW_HALO

    def chunk(c):
        return (n_chunks - 1 - c) if reverse else c

    def row_blk(b, c):
        return base_blk + b * n_chunks + chunk(c)

    def prev_blk(b, c):
        return jnp.maximum(row_blk(b, c) * halo_per_chunk - 1, 0)

    def next_blk(b, c):
        return jnp.minimum((row_blk(b, c) + 1) * halo_per_chunk, n_halo_blk - 1)

    full = lambda shape: pl.BlockSpec(shape, lambda b, c: (0,) * len(shape))
    tok = lambda width: pl.BlockSpec((RW_C, width), lambda b, c: (row_blk(b, c), 0))
    loc = lambda width: pl.BlockSpec((RW_C, width), lambda b, c: (b * n_chunks + chunk(c), 0))
    state = pl.BlockSpec((1, N_PAIR, LANES, LANES), lambda b, c: (b, 0, 0, 0))
    in_specs = [
        tok(RW_IN),
        pl.BlockSpec((RW_HALO, RW_IN), lambda b, c: (prev_blk(b, c), 0)),
        pl.BlockSpec((RW_HALO, RW_IN), lambda b, c: (next_blk(b, c), 0)),
        state,
        full((2, RW_IN)), full((1, C_RW)), full((2 * R_W, C_RW)), full((1, C_RW)), full((2 * R_A, C_RW)),
        full((1, C_RW)), full((1, C_RW)), full((1, C_RW)), full((RW_C, RW_C)),
    ]
    args = [p_rw, p_rw, p_rw, s0] + list(wts)
    if final:
        in_specs += [full((R_G, C_RW)), full((1, C_RW)), full((1, C_RW)), loc(C_RW), loc(C_RW)]
        args += list(extra)
        out_specs = [loc(C_RW), state]
        out_shape = [jax.ShapeDtypeStruct((batch * seq, C_RW), BF16),
                     jax.ShapeDtypeStruct((batch, N_PAIR, LANES, LANES), F32)]
    else:
        out_specs = [loc(C_RW), loc(C_RW), state]
        out_shape = [jax.ShapeDtypeStruct((batch * seq, C_RW), F32), jax.ShapeDtypeStruct((batch * seq, C_RW), F32),
                     jax.ShapeDtypeStruct((batch, N_PAIR, LANES, LANES), F32)]
    return pl.pallas_call(
        functools.partial(_rwkv_kernel, reverse, final, n_chunks),
        grid=(batch, n_chunks),
        in_specs=in_specs,
        out_specs=out_specs,
        out_shape=out_shape,
        scratch_shapes=[pltpu.VMEM((N_PAIR, LANES, LANES), F32)],
        compiler_params=pltpu.CompilerParams(
            dimension_semantics=("arbitrary", "arbitrary"), vmem_limit_bytes=VMEM_LIMIT),
        name="rwkv_bwd" if reverse else "rwkv_fwd",
    )(*args)


def _state_to_tiles(s):
    b = s.shape[0]
    st = jnp.swapaxes(s, -1, -2).reshape(b, N_PAIR, 2, HD, HD)
    z = jnp.zeros((b, N_PAIR, HD, HD), s.dtype)
    top = jnp.concatenate([st[:, :, 0], z], -1)
    bot = jnp.concatenate([z, st[:, :, 1]], -1)
    return jnp.concatenate([top, bot], -2)


def _tiles_to_state(t):
    b = t.shape[0]
    s_a = t[:, :, :HD, :HD]
    s_b = t[:, :, HD:, HD:]
    st = jnp.stack([s_a, s_b], 2).reshape(b, H_RW, HD, HD)
    return jnp.swapaxes(st, -1, -2)


def _rwkv_mix(p_rw, s0_fwd, s0_bwd, batch, seq, base_row, W):
    def dir_weights(d):
        pad = lambda m: jnp.zeros((2 * m.shape[1], C_RW), BF16).at[d * m.shape[1]:(d + 1) * m.shape[1]].set(
            m[d].astype(BF16))
        tri = np.triu(np.ones((RW_C, RW_C), np.float32)) if d == 1 else np.tril(np.ones((RW_C, RW_C), np.float32))
        return [W['rw_shift'], W['rw_w0'][d][None], pad(W['rw_w2']), W['rw_a0'][d][None], pad(W['rw_a2']),
                W['rw_ka'][None], W['rw_kk'][None], W['rw_rk'].reshape(1, C_RW), jnp.asarray(tri)]

    ys_b, bn_b, s_b = _rwkv_pass(True, False, batch, seq, base_row, p_rw, s0_bwd, dir_weights(1), None)
    extra = [W['rw_g2'].astype(BF16), W['rw_ln_g'][None], W['rw_ln_b'][None], ys_b, bn_b]
    y, s_f = _rwkv_pass(False, True, batch, seq, base_row, p_rw, s0_fwd, dir_weights(0), extra)
    return y, s_f, s_b


ATT_TQ = 256


def _attn_kernel(n_q, has_ctx, *refs):
    if has_ctx:
        q_ref, k_ref, v_ref, kc_ref, vc_ref, bias_ref, o_ref = refs
    else:
        q_ref, k_ref, v_ref, o_ref = refs
    lane = lax.broadcasted_iota(jnp.int32, (1, LANES), 1)
    m_a = lane < HD
    k = k_ref[...].astype(BF16)
    v = v_ref[...].astype(BF16)
    if has_ctx:
        kc = kc_ref[0].astype(BF16)
        vc = vc_ref[0].astype(BF16)
    for t in range(n_q // ATT_TQ):
        rows = slice(t * ATT_TQ, (t + 1) * ATT_TQ)
        q = q_ref[rows, :] * (HD ** -0.5)
        outs = []
        for h in range(2):
            q_h = jnp.where(m_a if h == 0 else jnp.logical_not(m_a), q, 0.0).astype(BF16)
            s = _dot_t(q_h, k)
            if has_ctx:
                s = s + bias_ref[h, rows, :]
                s_c = _dot_t(q_h, kc)
                m = jnp.maximum(jnp.max(s, -1, keepdims=True), jnp.max(s_c, -1, keepdims=True))
                e_c = jnp.exp(s_c - m)
                e = jnp.exp(s - m)
                den = jnp.sum(e, -1, keepdims=True) + jnp.sum(e_c, -1, keepdims=True)
                o = _dot(e.astype(BF16), v) + _dot(e_c.astype(BF16), vc)
            else:
                m = jnp.max(s, -1, keepdims=True)
                e = jnp.exp(s - m)
                den = jnp.sum(e, -1, keepdims=True)
                o = _dot(e.astype(BF16), v)
            outs.append(o / den)
        o_ref[rows, :] = jnp.where(m_a, outs[0], outs[1]).astype(o_ref.dtype)


def _attention(p_na, batch, seq, base_row, kc=None, vc=None, bias=None):
    has_ctx = kc is not None
    base_blk = base_row // seq
    tok = lambda off: pl.BlockSpec((seq, LANES), lambda p, b: (base_blk + b, off + p))
    in_specs = [tok(0), tok(N_PAIR), tok(2 * N_PAIR)]
    args = [p_na, p_na, p_na]
    if has_ctx:
        cache = pl.BlockSpec((1, PAST_LEN, LANES), lambda p, b: (b, 0, p))
        in_specs += [cache, cache, pl.BlockSpec((2, seq, seq), lambda p, b: (p, 0, 0))]
        args += [kc, vc, bias]
    return pl.pallas_call(
        functools.partial(_attn_kernel, seq, has_ctx),
        grid=(N_PAIR, batch),
        in_specs=in_specs,
        out_specs=pl.BlockSpec((seq, LANES), lambda p, b: (b, p)),
        out_shape=jax.ShapeDtypeStruct((batch * seq, C_NA), BF16),
        compiler_params=pltpu.CompilerParams(
            dimension_semantics=("arbitrary", "arbitrary"), vmem_limit_bytes=VMEM_LIMIT),
        name="natten_lat" if has_ctx else "attn_ctx",
    )(*args)


def _natten_bias(rpb):
    rows = DEC_SEQ // GRID_W
    kh = min(NA_KH_MAX, rows)
    qi = np.arange(rows)
    si = np.clip(qi - kh // 2, 0, rows - kh)
    ok_r = (qi[None, :] >= si[:, None]) & (qi[None, :] < si[:, None] + kh)
    idx_r = np.clip(qi[None, :] - qi[:, None] + NA_KH_MAX - 1, 0, 2 * NA_KH_MAX - 2)
    qj = np.arange(GRID_W)
    wj = np.clip(qj - NA_KW // 2, 0, GRID_W - NA_KW)
    ok_c = (qj[None, :] >= wj[:, None]) & (qj[None, :] < wj[:, None] + NA_KW)
    pad = GRID_W - NA_KW
    rp = jnp.pad(rpb, ((0, 0), (0, 0), (pad, pad)))
    toep = jnp.stack([rp[:, :, pad + NA_KW - 1 - j:pad + NA_KW - 1 - j + GRID_W] for j in range(GRID_W)], 2)
    toep = jnp.where(jnp.asarray(ok_c)[None, None], toep, NEG)
    neg_blk = jnp.full((H_NA, GRID_W, GRID_W), NEG, rpb.dtype)
    block_rows = [jnp.concatenate([toep[:, idx_r[i, k]] if ok_r[i, k] else neg_blk for k in range(rows)], -1)
                  for i in range(rows)]
    return jnp.stack(block_rows, 1).reshape(H_NA, DEC_SEQ, DEC_SEQ)


CV_PAD = 16
CV_TT = 256


def _conv_kernel(seq, val_ref, gate_ref, dw_ref, dwb_ref, lng_ref, lnb_ref, o_ref, u_scr, c_scr):
    zeros = jnp.zeros((CV_PAD, C_CV), F32)
    u_scr[0:CV_PAD, :] = zeros
    u_scr[seq + CV_PAD:seq + 2 * CV_PAD, :] = zeros
    u_scr[CV_PAD:seq + CV_PAD, :] = val_ref[...] * _sigmoid(gate_ref[...])
    off = CV_PAD - CONV_K // 2
    for cb in range(C_CV // LANES):
        cols = slice(cb * LANES, (cb + 1) * LANES)
        for t in range(seq // CV_TT):
            acc = jnp.zeros((CV_TT, LANES), F32)
            for j in range(CONV_K):
                acc = acc + dw_ref[j:j + 1, cols] * u_scr[t * CV_TT + off + j:t * CV_TT + off + j + CV_TT, cols]
            c_scr[t * CV_TT:(t + 1) * CV_TT, cols] = acc
    u = c_scr[...] + dwb_ref[...]
    mu = jnp.mean(u, -1, keepdims=True)
    d = u - mu
    var = jnp.mean(d * d, -1, keepdims=True)
    y = d * lax.rsqrt(var + LN_EPS) * lng_ref[...] + lnb_ref[...]
    o_ref[...] = (y * _sigmoid(y)).astype(o_ref.dtype)


def _conv_module(p_cv, batch, seq, base_row, W):
    base_blk = base_row // seq
    full = lambda shape: pl.BlockSpec(shape, lambda b: (0,) * len(shape))
    return pl.pallas_call(
        functools.partial(_conv_kernel, seq),
        grid=(batch,),
        in_specs=[
            pl.BlockSpec((seq, C_CV), lambda b: (base_blk + b, 0)),
            pl.BlockSpec((seq, C_CV), lambda b: (base_blk + b, 1)),
            full((CONV_K, C_CV)), full((1, C_CV)), full((1, C_CV)), full((1, C_CV)),
        ],
        out_specs=pl.BlockSpec((seq, C_CV), lambda b: (b, 0)),
        out_shape=jax.ShapeDtypeStruct((batch * seq, C_CV), BF16),
        scratch_shapes=[pltpu.VMEM((seq + 2 * CV_PAD, C_CV), F32), pltpu.VMEM((seq, C_CV), F32)],
        compiler_params=pltpu.CompilerParams(
            dimension_semantics=("arbitrary",), vmem_limit_bytes=VMEM_LIMIT),
        name="conv_module",
    )(p_cv, p_cv, W['cv_dw'], W['cv_dw_b'][None], W['cv_ln_g'][None], W['cv_ln_b'][None])


OUT_TM = 512


def _split_bf16(x):
    hi = x.astype(BF16)
    return hi, (x - hi.astype(F32)).astype(BF16)


def _out_kernel(yrw_ref, yna_ref, ycv_ref, x_ref, mods_ref, g_ref, w_ref, rt_ref, xo_ref, h_ref, lg_ref):
    y = jnp.concatenate([yrw_ref[...], yna_ref[...], ycv_ref[...]], -1)
    x_new = x_ref[...] + mods_ref[0, 2:3, :] * _dot(y, w_ref[0])
    xo_ref[...] = x_new
    h = _modulated_norm(x_new, g_ref[...], mods_ref[0, 3:4, :], mods_ref[0, 4:5, :])
    h_hi, h_lo = _split_bf16(h)
    h_ref[...] = h_hi
    rt_hi, rt_lo = _split_bf16(rt_ref[...])
    lg_ref[...] = _dot(h_hi, rt_hi) + (_dot(h_hi, rt_lo) + _dot(h_lo, rt_hi))


def _out_project(y_rw, y_na, y_cv, x, mods, g, layer, w_bf16, rt):
    per_seg = SEG // OUT_TM
    tok = lambda width: pl.BlockSpec((OUT_TM, width), lambda i: (i, 0))
    return pl.pallas_call(
        _out_kernel,
        grid=(N_TOK // OUT_TM,),
        in_specs=[
            tok(C_RW), tok(C_NA), tok(C_CV), tok(D_MODEL),
            pl.BlockSpec((1, 6, D_MODEL), lambda i: (i // per_seg, 0, 0)),
            pl.BlockSpec((1, D_MODEL), lambda i: (0, 0)),
            pl.BlockSpec((1, D_MODEL, D_MODEL), lambda i: (layer, 0, 0)),
            pl.BlockSpec((D_MODEL, RT_PAD), lambda i: (0, 0)),
        ],
        out_specs=[tok(D_MODEL), tok(D_MODEL), tok(RT_PAD)],
        out_shape=[
            jax.ShapeDtypeStruct((N_TOK, D_MODEL), F32),
            jax.ShapeDtypeStruct((N_TOK, D_MODEL), BF16),
            jax.ShapeDtypeStruct((N_TOK, RT_PAD), F32),
        ],
        compiler_params=pltpu.CompilerParams(
            dimension_semantics=("arbitrary",), vmem_limit_bytes=VMEM_LIMIT),
        name="out_proj",
    )(y_rw, y_na, y_cv, x, mods, g, w_bf16, rt)


def _moe_kernel(blk_e_ref, n_used_ref, xb_ref, w13_ref, w2_ref, o_ref):
    i = pl.program_id(0)

    @pl.when(i < n_used_ref[0])
    def _():
        gu = _dot(xb_ref[...], w13_ref[0, 0])
        gt = gu[:, :D_EXPERT]
        up = gu[:, D_EXPERT:]
        a = (gt * _sigmoid(gt) * up).astype(BF16)
        o_ref[...] = _dot(a, w2_ref[0, 0])

    @pl.when(i >= n_used_ref[0])
    def _():
        o_ref[...] = jnp.zeros_like(o_ref)


def _moe_blocks(blk_e, n_used, xb, layer, w13_bf16, w2_bf16):
    grid_spec = pltpu.PrefetchScalarGridSpec(
        num_scalar_prefetch=2,
        grid=(MOE_NBLK,),
        in_specs=[
            pl.BlockSpec((MOE_BM, D_MODEL), lambda i, be, nu: (i, 0)),
            pl.BlockSpec((1, 1, D_MODEL, 2 * D_EXPERT), lambda i, be, nu: (layer, be[i], 0, 0)),
            pl.BlockSpec((1, 1, D_EXPERT, D_MODEL), lambda i, be, nu: (layer, be[i], 0, 0)),
        ],
        out_specs=pl.BlockSpec((MOE_BM, D_MODEL), lambda i, be, nu: (i, 0)),
    )
    return pl.pallas_call(
        _moe_kernel,
        grid_spec=grid_spec,
        out_shape=jax.ShapeDtypeStruct((MOE_NBLK * MOE_BM, D_MODEL), F32),
        compiler_params=pltpu.CompilerParams(
            dimension_semantics=("arbitrary",), vmem_limit_bytes=VMEM_LIMIT),
        name="moe_blocks",
    )(blk_e, n_used, xb, w13_bf16, w2_bf16)


def _hmoe(h_bf16, logits, layer, w13_bf16, w2_bf16):
    n = N_TOK
    lg = logits[:, :N_GROUPS]
    grp = jnp.argmax(lg, -1).astype(jnp.int32)
    gate_g = jnp.take_along_axis(jax.nn.softmax(lg, -1), grp[:, None], -1)
    le = logits[:, N_GROUPS:N_GROUPS + N_EXPERTS].reshape(n, N_GROUPS, EXP_PER_GROUP)
    le = jnp.take_along_axis(le, grp[:, None, None], 1)[:, 0]
    top_l, top_i = lax.top_k(le, TOP_K)
    w = gate_g * jax.nn.softmax(top_l, -1)
    eid = grp[:, None] * EXP_PER_GROUP + top_i.astype(jnp.int32)

    a_tot = n * TOP_K
    rows = MOE_NBLK * MOE_BM
    flat_e = eid.reshape(-1)
    flat_t = jnp.arange(a_tot, dtype=jnp.int32) // TOP_K
    onehot = (flat_e[:, None] == jnp.arange(N_EXPERTS, dtype=jnp.int32)[None]).astype(jnp.int32)
    csum = jnp.cumsum(onehot, axis=0)
    counts = csum[-1]
    rank = jnp.take_along_axis(csum, flat_e[:, None], 1)[:, 0] - 1
    pcounts = (counts + MOE_BM - 1) // MOE_BM * MOE_BM
    pends = jnp.cumsum(pcounts)
    pstarts = pends - pcounts
    dest = pstarts[flat_e] + rank
    row_tok = jnp.zeros((rows,), jnp.int32).at[dest].set(flat_t)
    blk_start = jnp.arange(MOE_NBLK, dtype=jnp.int32) * MOE_BM
    n_used = (pends[-1] // MOE_BM).astype(jnp.int32)
    blk_e = jnp.searchsorted(pends, blk_start, side='right').astype(jnp.int32)
    last_e = jnp.searchsorted(pends, pends[-1] - 1, side='right').astype(jnp.int32)
    blk_e = jnp.minimum(blk_e, last_e)
    xb = h_bf16[row_tok]
    yb = _moe_blocks(blk_e, n_used.reshape(1), xb, layer, w13_bf16, w2_bf16)
    d2 = dest.reshape(n, TOP_K)
    return jnp.take(yb, d2[:, 0], axis=0), jnp.take(yb, d2[:, 1], axis=0), w


CMB_TM = 256


def _combine_kernel(final, x_ref, y0_ref, y1_ref, w_ref, mods_ref, nmods_ref, g_ref, *o_refs):
    moe = y0_ref[...] * w_ref[:, 0:1] + y1_ref[...] * w_ref[:, 1:2]
    x = x_ref[...] + mods_ref[0, 5:6, :] * moe
    if final:
        o_refs[0][...] = x * lax.rsqrt(jnp.mean(x * x, -1, keepdims=True) + NORM_EPS) * g_ref[...]
    else:
        o_refs[0][...] = x
        o_refs[1][...] = _modulated_norm(x, g_ref[...], nmods_ref[0, 0:1, :], nmods_ref[0, 1:2, :]).astype(BF16)


def _combine(final, x_new, y0, y1, w, mods, next_mods, g):
    per_seg = SEG // CMB_TM
    tok = pl.BlockSpec((CMB_TM, D_MODEL), lambda i: (i, 0))
    seg = pl.BlockSpec((1, 6, D_MODEL), lambda i: (i // per_seg, 0, 0))
    out_shape = [jax.ShapeDtypeStruct((N_TOK, D_MODEL), F32)]
    if not final:
        out_shape.append(jax.ShapeDtypeStruct((N_TOK, D_MODEL), BF16))
    return pl.pallas_call(
        functools.partial(_combine_kernel, final),
        grid=(N_TOK // CMB_TM,),
        in_specs=[tok, tok, tok, pl.BlockSpec((CMB_TM, TOP_K), lambda i: (i, 0)), seg, seg,
                  pl.BlockSpec((1, D_MODEL), lambda i: (0, 0))],
        out_specs=[tok] * len(out_shape),
        out_shape=out_shape,
        compiler_params=pltpu.CompilerParams(
            dimension_semantics=("arbitrary",), vmem_limit_bytes=VMEM_LIMIT),
        name="moe_combine",
    )(x_new, y0, y1, w, mods, next_mods, g)


def kernel(x_prompt, x_sample, cache_k, cache_v, state_rwkv, c, c_ctx, norm1_g, w_mod, b_mod, w_in,
           rw_shift, rw_w0, rw_w2, rw_a0, rw_a2, rw_g2, rw_kk, rw_ka, rw_rk, rw_ln_g, rw_ln_b, na_rpb,
           cv_dw, cv_dw_b, cv_ln_g, cv_ln_b, w_out, norm2_g, rt_group, rt_expert, ex_w13, ex_w2, final_g):
    x = jnp.concatenate([x_prompt.reshape(N_CTX_TOK, D_MODEL), x_sample.reshape(N_LAT_TOK, D_MODEL)], 0)
    cond = jnp.concatenate([c_ctx[None], c], 0)
    seg_cond = np.concatenate([np.zeros(N_CTX_SEG, np.int32), 1 + np.arange(DEC_BATCH, dtype=np.int32)])
    zero_state = jnp.zeros((BATCH, N_PAIR, LANES, LANES), F32)
    w_in16 = w_in.astype(BF16)
    w_rw16, w_na16, w_cv16 = w_in16[..., :RW_IN], w_in16[..., RW_IN:RW_IN + NA_IN], w_in16[..., RW_IN + NA_IN:]
    w_out16 = w_out.astype(BF16)
    w13_16 = ex_w13.astype(BF16)
    w2_16 = ex_w2.astype(BF16)
    mods_all = []
    for l in range(DEPTH):
        m = jax.nn.silu(cond) @ w_mod[l] + b_mod[l]
        mods_all.append(m.reshape(1 + DEC_BATCH, 6, D_MODEL)[seg_cond])
    h = _norm_mod(x, mods_all[0], norm1_g[0][None])
    ks_out, vs_out, ss_out = [], [], []
    for l in range(DEPTH):
        W = dict(rw_shift=rw_shift[l], rw_w0=rw_w0[l], rw_w2=rw_w2[l], rw_a0=rw_a0[l], rw_a2=rw_a2[l],
                 rw_g2=rw_g2[l], rw_kk=rw_kk[l], rw_ka=rw_ka[l], rw_rk=rw_rk[l], rw_ln_g=rw_ln_g[l],
                 rw_ln_b=rw_ln_b[l], cv_dw=cv_dw[l], cv_dw_b=cv_dw_b[l], cv_ln_g=cv_ln_g[l],
                 cv_ln_b=cv_ln_b[l])
        mods = mods_all[l]
        p_rw, p_na, p_cv = _project(h, l, w_rw16, w_na16, w_cv16)
        y_rw_c, s_f, s_b = _rwkv_mix(p_rw, zero_state, zero_state, BATCH, SEQ, 0, W)
        y_na_c = _attention(p_na, BATCH, SEQ, 0)
        y_cv_c = _conv_module(p_cv, BATCH, SEQ, 0, W)
        s0 = state_rwkv[:, l]
        y_rw_l, _, _ = _rwkv_mix(p_rw, _state_to_tiles(s0[:, 0]), _state_to_tiles(s0[:, 1]),
                                 DEC_BATCH, DEC_SEQ, N_CTX_TOK, W)
        y_na_l = _attention(p_na, DEC_BATCH, DEC_SEQ, N_CTX_TOK,
                            cache_k[:, l].reshape(DEC_BATCH, PAST_LEN, C_NA),
                            cache_v[:, l].reshape(DEC_BATCH, PAST_LEN, C_NA), _natten_bias(na_rpb[l]))
        y_cv_l = _conv_module(p_cv, DEC_BATCH, DEC_SEQ, N_CTX_TOK, W)
        y_rw = jnp.concatenate([y_rw_c, y_rw_l], 0)
        y_na = jnp.concatenate([y_na_c, y_na_l], 0)
        y_cv = jnp.concatenate([y_cv_c, y_cv_l], 0)
        ks_out.append(p_na[:N_CTX_TOK, C_NA:2 * C_NA].reshape(BATCH, SEQ, H_NA, HD))
        vs_out.append(p_na[:N_CTX_TOK, 2 * C_NA:].reshape(BATCH, SEQ, H_NA, HD))
        ss_out.append(jnp.stack([_tiles_to_state(s_f), _tiles_to_state(s_b)], 1))
        rt = jnp.concatenate([rt_group[l], rt_expert[l],
                              jnp.zeros((D_MODEL, RT_PAD - N_GROUPS - N_EXPERTS), F32)], 1)
        x_new, h2, logits = _out_project(y_rw, y_na, y_cv, x, mods, norm2_g[l][None], l, w_out16, rt)
        y0, y1, w_tok = _hmoe(h2, logits, l, w13_16, w2_16)
        if l == DEPTH - 1:
            (y,) = _combine(True, x_new, y0, y1, w_tok, mods, mods, final_g[None])
        else:
            x, h = _combine(False, x_new, y0, y1, w_tok, mods, mods_all[l + 1], norm1_g[l + 1][None])
    y_prompt = y[:N_CTX_TOK].reshape(BATCH, SEQ, D_MODEL)
    y_sample = y[N_CTX_TOK:].reshape(DEC_BATCH, DEC_SEQ, D_MODEL)
    return (y_prompt, y_sample, jnp.stack(ks_out, axis=1), jnp.stack(vs_out, axis=1),
            jnp.stack(ss_out, axis=1))
```

```python
import functools

import jax
import jax.numpy as jnp
import numpy as np
from jax import lax
from jax.experimental import pallas as pl
from jax.experimental.pallas import tpu as pltpu

D_MODEL = 2048
BATCH = 16
SEQ = 256
DEPTH = 4
DEC_BATCH = 8
DEC_SEQ = 1024
PAST_LEN = 512

GRID_W = 64
HD = 64
H_RW = 12
C_RW = H_RW * HD
H_NA = 12
C_NA = H_NA * HD
C_CV = D_MODEL - C_RW - C_NA
R_W = 64
R_A = 64
R_G = 128
RW_IN = 3 * C_RW + 2 * R_W + 2 * R_A + R_G
NA_IN = 3 * C_NA
CV_IN = 2 * C_CV
NA_KH_MAX = 8
NA_KW = 16
CONV_K = 31
N_GROUPS = 4
EXP_PER_GROUP = 8
N_EXPERTS = N_GROUPS * EXP_PER_GROUP
TOP_K = 2
D_EXPERT = 1024
NORM_EPS = 1e-6
LN_EPS = 1e-5
RW_GN_EPS = 64e-5
NEG = -1e30

LANES = 128
N_PAIR = H_RW // 2
SEG = 1024
N_CTX_TOK = BATCH * SEQ
N_LAT_TOK = DEC_BATCH * DEC_SEQ
N_TOK = N_CTX_TOK + N_LAT_TOK
N_SEG = N_TOK // SEG
N_CTX_SEG = N_CTX_TOK // SEG
RT_PAD = LANES
MOE_BM = 256
MOE_NBLK = (N_TOK * TOP_K) // MOE_BM + N_EXPERTS
VMEM_LIMIT = 48 * 1024 * 1024

BF16 = jnp.bfloat16
F32 = jnp.float32


def _dot(a, b):
    return jnp.dot(a, b, preferred_element_type=F32)


def _dot_t(a, b):
    return lax.dot_general(a, b, (((1,), (1,)), ((), ())), preferred_element_type=F32)


def _sigmoid(x):
    return 1.0 / (1.0 + jnp.exp(-x))


def _head_sum(x, m_a):
    s_a = jnp.sum(jnp.where(m_a, x, 0.0), -1, keepdims=True)
    s_b = jnp.sum(jnp.where(m_a, 0.0, x), -1, keepdims=True)
    return jnp.where(m_a, s_a, s_b)


MOD_ROWS = 16
MOD_TN = 1024


def _mods_kernel(c_ref, w_ref, b_ref, o_ref):
    c = c_ref[...]
    o_ref[0] = _dot((c * _sigmoid(c)).astype(BF16), w_ref[0].astype(BF16)) + b_ref[0]


def _adaln_mods(cond, w_mod, b_mod):
    return pl.pallas_call(
        _mods_kernel,
        grid=(DEPTH, 6 * D_MODEL // MOD_TN),
        in_specs=[
            pl.BlockSpec((MOD_ROWS, D_MODEL), lambda l, j: (0, 0)),
            pl.BlockSpec((1, D_MODEL, MOD_TN), lambda l, j: (l, 0, j)),
            pl.BlockSpec((1, 1, MOD_TN), lambda l, j: (l, 0, j)),
        ],
        out_specs=pl.BlockSpec((1, MOD_ROWS, MOD_TN), lambda l, j: (l, 0, j)),
        out_shape=jax.ShapeDtypeStruct((DEPTH, MOD_ROWS, 6 * D_MODEL), F32),
        compiler_params=pltpu.CompilerParams(
            dimension_semantics=("arbitrary", "arbitrary"), vmem_limit_bytes=VMEM_LIMIT),
        name="adaln_mods",
    )(cond, w_mod, b_mod)


NORM_TM = 256


def _modulated_norm(x, g, shift, scale):
    y = x * lax.rsqrt(jnp.mean(x * x, -1, keepdims=True) + NORM_EPS)
    return (y * g) * (1.0 + scale) + shift


def _norm_kernel(x_ref, mods_ref, g_ref, h_ref):
    h_ref[...] = _modulated_norm(x_ref[...], g_ref[...], mods_ref[0, 0:1, :], mods_ref[0, 1:2, :]).astype(BF16)


def _norm_mod(x, mods, g):
    tok = pl.BlockSpec((NORM_TM, D_MODEL), lambda i: (i, 0))
    return pl.pallas_call(
        _norm_kernel,
        grid=(N_TOK // NORM_TM,),
        in_specs=[tok, pl.BlockSpec((1, 6, D_MODEL), lambda i: (i // (SEG // NORM_TM), 0, 0)),
                  pl.BlockSpec((1, D_MODEL), lambda i: (0, 0))],
        out_specs=tok,
        out_shape=jax.ShapeDtypeStruct((N_TOK, D_MODEL), BF16),
        compiler_params=pltpu.CompilerParams(
            dimension_semantics=("arbitrary",), vmem_limit_bytes=VMEM_LIMIT),
        name="norm_mod",
    )(x, mods, g)


PROJ_TM = SEG
PROJ_RW_TN = RW_IN // 3
PROJ_NA_TN = NA_IN // 3
PROJ_CV_TN = CV_IN // 2
PROJ_STEPS = 8


def _proj_kernel(h_ref, wrw_ref, wna_ref, wcv_ref, orw_ref, ona_ref, ocv_ref):
    j = pl.program_id(1)

    @pl.when(j < 3)
    def _():
        orw_ref[...] = _dot(h_ref[...], wrw_ref[0])

    @pl.when((j >= 3) & (j < 6))
    def _():
        ona_ref[...] = _dot(h_ref[...], wna_ref[0])

    @pl.when(j >= 6)
    def _():
        ocv_ref[...] = _dot(h_ref[...], wcv_ref[0])


def _project(h, layer, w_rw, w_na, w_cv):
    def c_rw(j):
        return jnp.minimum(j, 2)

    def c_na(j):
        return jnp.clip(j - 3, 0, 2)

    def c_cv(j):
        return jnp.clip(j - 6, 0, 1)

    return pl.pallas_call(
        _proj_kernel,
        grid=(N_TOK // PROJ_TM, PROJ_STEPS),
        in_specs=[
            pl.BlockSpec((PROJ_TM, D_MODEL), lambda i, j: (i, 0)),
            pl.BlockSpec((1, D_MODEL, PROJ_RW_TN), lambda i, j: (layer, 0, c_rw(j))),
            pl.BlockSpec((1, D_MODEL, PROJ_NA_TN), lambda i, j: (layer, 0, c_na(j))),
            pl.BlockSpec((1, D_MODEL, PROJ_CV_TN), lambda i, j: (layer, 0, c_cv(j))),
        ],
        out_specs=[
            pl.BlockSpec((PROJ_TM, PROJ_RW_TN), lambda i, j: (i, c_rw(j))),
            pl.BlockSpec((PROJ_TM, PROJ_NA_TN), lambda i, j: (i, c_na(j))),
            pl.BlockSpec((PROJ_TM, PROJ_CV_TN), lambda i, j: (i, c_cv(j))),
        ],
        out_shape=[
            jax.ShapeDtypeStruct((N_TOK, RW_IN), F32),
            jax.ShapeDtypeStruct((N_TOK, NA_IN), F32),
            jax.ShapeDtypeStruct((N_TOK, CV_IN), F32),
        ],
        compiler_params=pltpu.CompilerParams(
            dimension_semantics=("arbitrary", "arbitrary"), vmem_limit_bytes=VMEM_LIMIT),
        name="in_proj",
    )(h, w_rw, w_na, w_cv)


RW_C = 64
RW_NCH = 4
RW_BLK = RW_NCH * RW_C
RW_HALO = 8
RW_SB = 16
RW_SB_SQUARINGS = 3
assert RW_C == 4 * RW_SB


def _rwkv_kernel(reverse, final, n_blocks, *refs):
    if final:
        (p_ref, pp_ref, pn_ref, s0_ref, shift_ref, w0_ref, w2_ref, a0_ref, a2_ref, ka_ref, kkw_ref, rk_ref,
         tri_ref, g2_ref, lng_ref, lnb_ref, ysin_ref, bnin_ref, y_ref, sfin_ref, s_scr) = refs
    else:
        (p_ref, pp_ref, pn_ref, s0_ref, shift_ref, w0_ref, w2_ref, a0_ref, a2_ref, ka_ref, kkw_ref, rk_ref,
         tri_ref, ys_ref, bn_ref, sfin_ref, s_scr) = refs
    c = pl.program_id(1)
    cc = (n_blocks - 1 - c) if reverse else c

    @pl.when(c == 0)
    def _():
        s_scr[...] = s0_ref[0]

    x = p_ref[...]
    row = lax.broadcasted_iota(jnp.int32, (RW_BLK, 1), 0)
    hp = jnp.where(cc > 0, pp_ref[RW_HALO - 1:RW_HALO, :], 0.0)
    hn = jnp.where(cc < n_blocks - 1, pn_ref[0:1, :], 0.0)
    prev = jnp.where(row == 0, hp, pltpu.roll(x, 1, 0))
    nxt = jnp.where(row == RW_BLK - 1, hn, pltpu.roll(x, RW_BLK - 1, 0))
    xs = x + shift_ref[0:1, :] * (prev - x) + shift_ref[1:2, :] * (nxt - x)

    r = xs[:, 0:C_RW]
    k = xs[:, C_RW:2 * C_RW]
    v = xs[:, 2 * C_RW:3 * C_RW]
    wd = xs[:, 3 * C_RW:3 * C_RW + 2 * R_W]
    ad = xs[:, 3 * C_RW + 2 * R_W:3 * C_RW + 2 * R_W + 2 * R_A]
    wl = w0_ref[...] + _dot(jnp.tanh(wd).astype(BF16), w2_ref[...])
    softplus_neg = jnp.maximum(-wl, 0.0) + jnp.log(1.0 + jnp.exp(-jnp.abs(wl)))
    lw = -jnp.exp(-softplus_neg - 0.5)
    a_sig = _sigmoid(a0_ref[...] + _dot(ad.astype(BF16), a2_ref[...]))
    kmod = k * (1.0 + (a_sig - 1.0) * ka_ref[...])
    kkf = k * kkw_ref[...]
    bonus_w = r * kmod * rk_ref[...]
    cl = jnp.dot(tri_ref[...], lw, preferred_element_type=F32, precision=lax.Precision.HIGHEST)
    end_rows = [j * RW_C if reverse else (j + 1) * RW_C - 1 for j in range(RW_NCH)]
    cl_ends = [cl[e:e + 1, :] for e in end_rows]
    cl_end = jnp.concatenate([jnp.broadcast_to(ce, (RW_C, C_RW)) for ce in cl_ends], 0)
    g_in = jnp.exp(cl)
    g_inv = jnp.exp(-cl)
    g_prev = jnp.exp(cl - lw)
    g_end = jnp.exp(cl_end - cl)
    g_tot = [jnp.exp(ce) for ce in cl_ends]

    lane = lax.broadcasted_iota(jnp.int32, (1, LANES), 1)
    m_a = lane < HD
    ri = lax.broadcasted_iota(jnp.int32, (2 * RW_C, 2 * RW_C), 0)
    ci = lax.broadcasted_iota(jnp.int32, (2 * RW_C, 2 * RW_C), 1)
    strict = (ri < ci) if reverse else (ri > ci)
    incl = (ri <= ci) if reverse else (ri >= ci)
    eye = ri == ci
    sub_blk = (ri // RW_SB) == (ci // RW_SB)

    def stack(t):
        return jnp.concatenate([jnp.where(m_a, t, 0.0), jnp.where(m_a, 0.0, t)], 0)

    if final:
        gd = xs[:, 3 * C_RW + 2 * R_W + 2 * R_A:]
        gate = _dot(_sigmoid(gd).astype(BF16), g2_ref[...])

    n = 2 * RW_C
    sls = [slice(p * LANES, (p + 1) * LANES) for p in range(N_PAIR)]
    rws = [slice(j * RW_C, (j + 1) * RW_C) for j in range(RW_NCH)]
    items = [(j, p) for j in range(RW_NCH) for p in range(N_PAIR)]
    pairs = range(len(items))
    z_a, z_r, z_v, z_bh, z_kh, bonus = [], [], [], [], [], []
    l_ab, l_ak, t_rb, t_rk = [], [], [], []
    for j, p in items:
        rs, sl = rws[j], sls[p]
        r_p, k_p, v_p, a_p = r[rs, sl], kmod[rs, sl], v[rs, sl], a_sig[rs, sl]
        kkf_p = kkf[rs, sl]
        kk = kkf_p * lax.rsqrt(jnp.maximum(_head_sum(kkf_p * kkf_p, m_a), 1e-24))
        bb = kk * a_p
        bonus.append(_head_sum(bonus_w[rs, sl], m_a) * v_p)
        z_a.append(stack(-kk * g_prev[rs, sl]))
        z_r.append(stack(r_p * g_in[rs, sl]))
        z_b = stack(bb * g_inv[rs, sl])
        z_k = stack(k_p * g_inv[rs, sl])
        z_bh.append(stack(bb * g_end[rs, sl]))
        z_kh.append(stack(k_p * g_end[rs, sl]))
        z_v.append(stack(v_p))
        gram = _dot_t(jnp.concatenate([z_a[-1], z_r[-1]], 0).astype(BF16),
                      jnp.concatenate([z_b, z_k], 0).astype(BF16))
        l_ab.append(jnp.where(strict, gram[:n, :n], 0.0).astype(BF16))
        l_ak.append(jnp.where(strict, gram[:n, n:], 0.0).astype(BF16))
        t_rb.append(jnp.where(incl, gram[n:, :n], 0.0))
        t_rk.append(jnp.where(incl, gram[n:, n:], 0.0))
    xx = [jnp.concatenate([z_a[p], _dot(l_ak[p], z_v[p].astype(BF16))], 1) for p in pairs]
    pw = [jnp.where(sub_blk, l_ab[p], 0.0) for p in pairs]
    n_off = [jnp.where(sub_blk, 0.0, l_ab[p]) for p in pairs]
    td = [jnp.where(eye, 1.0, pw[p].astype(F32)) for p in pairs]
    for _ in range(RW_SB_SQUARINGS):
        pw = [_dot(pw[p], pw[p]).astype(BF16) for p in pairs]
        td = [td[p] + _dot(td[p].astype(BF16), pw[p]) for p in pairs]
    td = [td[p].astype(BF16) for p in pairs]
    mm = [_dot(td[p], n_off[p]).astype(BF16) for p in pairs]
    xx = [_dot(td[p], xx[p].astype(BF16)) for p in pairs]
    xx = [xx[p] + _dot(mm[p], xx[p].astype(BF16)) for p in pairs]
    mm = [_dot(mm[p], mm[p]).astype(BF16) for p in pairs]
    xx = [xx[p] + _dot(mm[p], xx[p].astype(BF16)) for p in pairs]
    res = []
    for p in pairs:
        lhs = jnp.concatenate([jnp.concatenate([t_rb[p], t_rk[p]], 1),
                               jnp.concatenate([z_bh[p].T, z_kh[p].T], 1)], 0).astype(BF16)
        rhs = jnp.concatenate([xx[p], jnp.concatenate([jnp.zeros((n, LANES), F32), z_v[p]], 1)], 0).astype(BF16)
        res.append(_dot(lhs, rhs))
    state = [s_scr[p] for p in range(N_PAIR)]
    for j in (reversed(range(RW_NCH)) if reverse else range(RW_NCH)):
        for p in range(N_PAIR):
            i, rs, sl = j * N_PAIR + p, rws[j], sls[p]
            q_m = z_r[i] + res[i][:n, :LANES]
            g_m = jnp.where(eye, g_tot[j][:, sl], 0.0) + res[i][n:, :LANES]
            upd = _dot(jnp.concatenate([q_m, g_m], 0).astype(BF16), state[p].astype(BF16))
            ys = upd[:n] + res[i][:n, LANES:]
            state[p] = upd[n:] + res[i][n:, LANES:]
            y_p = ys[:RW_C] + ys[RW_C:]
            if final:
                y_t = y_p + ysin_ref[rs, sl]
                mu = _head_sum(y_t, m_a) * (1.0 / HD)
                d = y_t - mu
                var = _head_sum(d * d, m_a) * (1.0 / HD)
                y_n = d * lax.rsqrt(var + RW_GN_EPS) * lng_ref[:, sl] + lnb_ref[:, sl]
                y_ref[rs, sl] = ((y_n + bonus[i] + bnin_ref[rs, sl]) * gate[rs, sl]).astype(y_ref.dtype)
            else:
                ys_ref[rs, sl] = y_p
                bn_ref[rs, sl] = bonus[i]
    for p in range(N_PAIR):
        s_scr[p] = state[p]

    @pl.when(c == n_blocks - 1)
    def _():
        sfin_ref[0] = s_scr[...]


def _rwkv_pass(reverse, final, batch, seq, base_row, p_rw, s0, wts, extra):
    n_blocks = seq // RW_BLK
    base_blk = base_row // RW_BLK
    halo_per_chunk = RW_BLK // RW_HALO
    n_halo_blk = N_TOK // RW_HALO

    def chunk(c):
        return (n_blocks - 1 - c) if reverse else c

    def row_blk(b, c):
        return base_blk + b * n_blocks + chunk(c)

    def prev_blk(b, c):
        return jnp.maximum(row_blk(b, c) * halo_per_chunk - 1, 0)

    def next_blk(b, c):
        return jnp.minimum((row_blk(b, c) + 1) * halo_per_chunk, n_halo_blk - 1)

    full = lambda shape: pl.BlockSpec(shape, lambda b, c: (0,) * len(shape))
    tok = lambda width: pl.BlockSpec((RW_BLK, width), lambda b, c: (row_blk(b, c), 0))
    loc = lambda width: pl.BlockSpec((RW_BLK, width), lambda b, c: (b * n_blocks + chunk(c), 0))
    state = pl.BlockSpec((1, N_PAIR, LANES, LANES), lambda b, c: (b, 0, 0, 0))
    in_specs = [
        tok(RW_IN),
        pl.BlockSpec((RW_HALO, RW_IN), lambda b, c: (prev_blk(b, c), 0)),
        pl.BlockSpec((RW_HALO, RW_IN), lambda b, c: (next_blk(b, c), 0)),
        state,
        full((2, RW_IN)), full((1, C_RW)), full((2 * R_W, C_RW)), full((1, C_RW)), full((2 * R_A, C_RW)),
        full((1, C_RW)), full((1, C_RW)), full((1, C_RW)), full((RW_BLK, RW_BLK)),
    ]
    args = [p_rw, p_rw, p_rw, s0] + list(wts)
    if final:
        in_specs += [full((R_G, C_RW)), full((1, C_RW)), full((1, C_RW)), loc(C_RW), loc(C_RW)]
        args += list(extra)
        out_specs = [loc(C_RW), state]
        out_shape = [jax.ShapeDtypeStruct((batch * seq, C_RW), BF16),
                     jax.ShapeDtypeStruct((batch, N_PAIR, LANES, LANES), F32)]
    else:
        out_specs = [loc(C_RW), loc(C_RW), state]
        out_shape = [jax.ShapeDtypeStruct((batch * seq, C_RW), F32), jax.ShapeDtypeStruct((batch * seq, C_RW), F32),
                     jax.ShapeDtypeStruct((batch, N_PAIR, LANES, LANES), F32)]
    return pl.pallas_call(
        functools.partial(_rwkv_kernel, reverse, final, n_blocks),
        grid=(batch, n_blocks),
        in_specs=in_specs,
        out_specs=out_specs,
        out_shape=out_shape,
        scratch_shapes=[pltpu.VMEM((N_PAIR, LANES, LANES), F32)],
        compiler_params=pltpu.CompilerParams(
            dimension_semantics=("arbitrary", "arbitrary"), vmem_limit_bytes=VMEM_LIMIT),
        name="rwkv_bwd" if reverse else "rwkv_fwd",
    )(*args)


def _state_to_tiles(s):
    b = s.shape[0]
    st = jnp.swapaxes(s, -1, -2).reshape(b, N_PAIR, 2, HD, HD)
    z = jnp.zeros((b, N_PAIR, HD, HD), s.dtype)
    top = jnp.concatenate([st[:, :, 0], z], -1)
    bot = jnp.concatenate([z, st[:, :, 1]], -1)
    return jnp.concatenate([top, bot], -2)


def _tiles_to_state(t):
    b = t.shape[0]
    s_a = t[:, :, :HD, :HD]
    s_b = t[:, :, HD:, HD:]
    st = jnp.stack([s_a, s_b], 2).reshape(b, H_RW, HD, HD)
    return jnp.swapaxes(st, -1, -2)


def _rwkv_mix(p_rw, s0_fwd, s0_bwd, batch, seq, base_row, W):
    def dir_weights(d):
        pad = lambda m: jnp.zeros((2 * m.shape[1], C_RW), BF16).at[d * m.shape[1]:(d + 1) * m.shape[1]].set(
            m[d].astype(BF16))
        tri = np.triu(np.ones((RW_C, RW_C), np.float32)) if d == 1 else np.tril(np.ones((RW_C, RW_C), np.float32))
        tri = np.kron(np.eye(RW_NCH, dtype=np.float32), tri)
        return [W['rw_shift'], W['rw_w0'][d][None], pad(W['rw_w2']), W['rw_a0'][d][None], pad(W['rw_a2']),
                W['rw_ka'][None], W['rw_kk'][None], W['rw_rk'].reshape(1, C_RW), jnp.asarray(tri)]

    ys_b, bn_b, s_b = _rwkv_pass(True, False, batch, seq, base_row, p_rw, s0_bwd, dir_weights(1), None)
    extra = [W['rw_g2'].astype(BF16), W['rw_ln_g'][None], W['rw_ln_b'][None], ys_b, bn_b]
    y, s_f = _rwkv_pass(False, True, batch, seq, base_row, p_rw, s0_fwd, dir_weights(0), extra)
    return y, s_f, s_b


ATT_TQ = 256


def _attn_kernel(n_q, has_ctx, *refs):
    if has_ctx:
        q_ref, k_ref, v_ref, kc_ref, vc_ref, bias_ref, o_ref = refs
    else:
        q_ref, k_ref, v_ref, o_ref = refs
    lane = lax.broadcasted_iota(jnp.int32, (1, LANES), 1)
    m_a = lane < HD
    k = k_ref[...].astype(BF16)
    v = v_ref[...].astype(BF16)
    if has_ctx:
        kc = kc_ref[0, 0].astype(BF16)
        vc = vc_ref[0, 0].astype(BF16)
    for t in range(n_q // ATT_TQ):
        rows = slice(t * ATT_TQ, (t + 1) * ATT_TQ)
        q = q_ref[rows, :] * (HD ** -0.5)
        outs = []
        for h in range(2):
            q_h = jnp.where(m_a if h == 0 else jnp.logical_not(m_a), q, 0.0).astype(BF16)
            s = _dot_t(q_h, k)
            if has_ctx:
                s = s + bias_ref[0, h, rows, :]
                s_c = _dot_t(q_h, kc)
                m = jnp.maximum(jnp.max(s, -1, keepdims=True), jnp.max(s_c, -1, keepdims=True))
                e_c = jnp.exp(s_c - m)
                e = jnp.exp(s - m)
                den = jnp.sum(e, -1, keepdims=True) + jnp.sum(e_c, -1, keepdims=True)
                o = _dot(e.astype(BF16), v) + _dot(e_c.astype(BF16), vc)
            else:
                m = jnp.max(s, -1, keepdims=True)
                e = jnp.exp(s - m)
                den = jnp.sum(e, -1, keepdims=True)
                o = _dot(e.astype(BF16), v)
            outs.append(o / den)
        o_ref[rows, :] = jnp.where(m_a, outs[0], outs[1]).astype(o_ref.dtype)


def _attention(p_na, batch, seq, base_row, layer=None, kc=None, vc=None, bias=None):
    has_ctx = kc is not None
    base_blk = base_row // seq
    tok = lambda off: pl.BlockSpec((seq, LANES), lambda p, b: (base_blk + b, off + p))
    in_specs = [tok(0), tok(N_PAIR), tok(2 * N_PAIR)]
    args = [p_na, p_na, p_na]
    if has_ctx:
        cache = pl.BlockSpec((1, 1, PAST_LEN, LANES), lambda p, b: (b, layer, 0, p))
        in_specs += [cache, cache, pl.BlockSpec((1, 2, seq, seq), lambda p, b: (layer, p, 0, 0))]
        args += [kc, vc, bias]
    return pl.pallas_call(
        functools.partial(_attn_kernel, seq, has_ctx),
        grid=(N_PAIR, batch),
        in_specs=in_specs,
        out_specs=pl.BlockSpec((seq, LANES), lambda p, b: (b, p)),
        out_shape=jax.ShapeDtypeStruct((batch * seq, C_NA), BF16),
        compiler_params=pltpu.CompilerParams(
            dimension_semantics=("arbitrary", "arbitrary"), vmem_limit_bytes=VMEM_LIMIT),
        name="natten_lat" if has_ctx else "attn_ctx",
    )(*args)


def _natten_bias(rpb):
    rows = DEC_SEQ // GRID_W
    kh = min(NA_KH_MAX, rows)
    qi = np.arange(rows)
    si = np.clip(qi - kh // 2, 0, rows - kh)
    ok_r = (qi[None, :] >= si[:, None]) & (qi[None, :] < si[:, None] + kh)
    idx_r = np.clip(qi[None, :] - qi[:, None] + NA_KH_MAX - 1, 0, 2 * NA_KH_MAX - 2)
    qj = np.arange(GRID_W)
    wj = np.clip(qj - NA_KW // 2, 0, GRID_W - NA_KW)
    ok_c = (qj[None, :] >= wj[:, None]) & (qj[None, :] < wj[:, None] + NA_KW)
    pad = GRID_W - NA_KW
    lead = rpb.shape[:-2]
    rp = jnp.pad(rpb, ((0, 0),) * (rpb.ndim - 1) + ((pad, pad),))
    toep = jnp.stack([rp[..., pad + NA_KW - 1 - j:pad + NA_KW - 1 - j + GRID_W] for j in range(GRID_W)], -2)
    toep = jnp.where(jnp.asarray(ok_c), toep, NEG)
    neg_blk = jnp.full(lead + (GRID_W, GRID_W), NEG, rpb.dtype)
    block_rows = [jnp.concatenate([toep[..., idx_r[i, k], :, :] if ok_r[i, k] else neg_blk for k in range(rows)], -1)
                  for i in range(rows)]
    return jnp.stack(block_rows, -3).reshape(lead + (DEC_SEQ, DEC_SEQ))


CV_PAD = 16
CV_TT = 256


def _conv_kernel(seq, val_ref, gate_ref, dw_ref, dwb_ref, lng_ref, lnb_ref, o_ref, u_scr, c_scr):
    zeros = jnp.zeros((CV_PAD, C_CV), F32)
    u_scr[0:CV_PAD, :] = zeros
    u_scr[seq + CV_PAD:seq + 2 * CV_PAD, :] = zeros
    u_scr[CV_PAD:seq + CV_PAD, :] = val_ref[...] * _sigmoid(gate_ref[...])
    off = CV_PAD - CONV_K // 2
    for cb in range(C_CV // LANES):
        cols = slice(cb * LANES, (cb + 1) * LANES)
        for t in range(seq // CV_TT):
            acc = jnp.zeros((CV_TT, LANES), F32)
            for j in range(CONV_K):
                acc = acc + dw_ref[j:j + 1, cols] * u_scr[t * CV_TT + off + j:t * CV_TT + off + j + CV_TT, cols]
            c_scr[t * CV_TT:(t + 1) * CV_TT, cols] = acc
    u = c_scr[...] + dwb_ref[...]
    mu = jnp.mean(u, -1, keepdims=True)
    d = u - mu
    var = jnp.mean(d * d, -1, keepdims=True)
    y = d * lax.rsqrt(var + LN_EPS) * lng_ref[...] + lnb_ref[...]
    o_ref[...] = (y * _sigmoid(y)).astype(o_ref.dtype)


def _conv_module(p_cv, batch, seq, base_row, W):
    base_blk = base_row // seq
    full = lambda shape: pl.BlockSpec(shape, lambda b: (0,) * len(shape))
    return pl.pallas_call(
        functools.partial(_conv_kernel, seq),
        grid=(batch,),
        in_specs=[
            pl.BlockSpec((seq, C_CV), lambda b: (base_blk + b, 0)),
            pl.BlockSpec((seq, C_CV), lambda b: (base_blk + b, 1)),
            full((CONV_K, C_CV)), full((1, C_CV)), full((1, C_CV)), full((1, C_CV)),
        ],
        out_specs=pl.BlockSpec((seq, C_CV), lambda b: (b, 0)),
        out_shape=jax.ShapeDtypeStruct((batch * seq, C_CV), BF16),
        scratch_shapes=[pltpu.VMEM((seq + 2 * CV_PAD, C_CV), F32), pltpu.VMEM((seq, C_CV), F32)],
        compiler_params=pltpu.CompilerParams(
            dimension_semantics=("arbitrary",), vmem_limit_bytes=VMEM_LIMIT),
        name="conv_module",
    )(p_cv, p_cv, W['cv_dw'], W['cv_dw_b'][None], W['cv_ln_g'][None], W['cv_ln_b'][None])


OUT_TM = 512
OUT_CTX_STEPS = N_CTX_TOK // OUT_TM


def _split_bf16(x):
    hi = x.astype(BF16)
    return hi, (x - hi.astype(F32)).astype(BF16)


def _out_kernel(yrw_c, yna_c, ycv_c, yrw_l, yna_l, ycv_l, x_ref, mods_ref, g_ref, w_ref, rt_ref,
                xo_ref, h_ref, lg_ref, y_scr):
    i = pl.program_id(0)

    @pl.when(i < OUT_CTX_STEPS)
    def _():
        y_scr[...] = jnp.concatenate([yrw_c[...], yna_c[...], ycv_c[...]], -1)

    @pl.when(i >= OUT_CTX_STEPS)
    def _():
        y_scr[...] = jnp.concatenate([yrw_l[...], yna_l[...], ycv_l[...]], -1)

    x_new = x_ref[...] + mods_ref[0, 2:3, :] * _dot(y_scr[...], w_ref[0])
    xo_ref[...] = x_new
    h = _modulated_norm(x_new, g_ref[...], mods_ref[0, 3:4, :], mods_ref[0, 4:5, :])
    h_hi, h_lo = _split_bf16(h)
    bits = lax.bitcast_convert_type(h_hi.astype(F32), jnp.int32)
    h_ref[...] = bits[:, :D_MODEL // 2] | lax.shift_right_logical(bits[:, D_MODEL // 2:], 16)
    rt_hi, rt_lo = _split_bf16(rt_ref[...])
    lg_ref[...] = _dot(h_hi, rt_hi) + (_dot(h_hi, rt_lo) + _dot(h_lo, rt_hi))


def _out_project(y_ctx, y_lat, x, mods, g, layer, w_bf16, rt):
    per_seg = SEG // OUT_TM
    tok = lambda width: pl.BlockSpec((OUT_TM, width), lambda i: (i, 0))
    ctx = lambda width: pl.BlockSpec((OUT_TM, width), lambda i: (jnp.minimum(i, OUT_CTX_STEPS - 1), 0))
    lat = lambda width: pl.BlockSpec((OUT_TM, width), lambda i: (jnp.maximum(i - OUT_CTX_STEPS, 0), 0))
    return pl.pallas_call(
        _out_kernel,
        grid=(N_TOK // OUT_TM,),
        in_specs=[
            ctx(C_RW), ctx(C_NA), ctx(C_CV), lat(C_RW), lat(C_NA), lat(C_CV), tok(D_MODEL),
            pl.BlockSpec((1, 6, D_MODEL), lambda i: (i // per_seg, 0, 0)),
            pl.BlockSpec((1, D_MODEL), lambda i: (0, 0)),
            pl.BlockSpec((1, D_MODEL, D_MODEL), lambda i: (layer, 0, 0)),
            pl.BlockSpec((D_MODEL, RT_PAD), lambda i: (0, 0)),
        ],
        out_specs=[tok(D_MODEL), tok(D_MODEL // 2), tok(RT_PAD)],
        out_shape=[
            jax.ShapeDtypeStruct((N_TOK, D_MODEL), F32),
            jax.ShapeDtypeStruct((N_TOK, D_MODEL // 2), jnp.int32),
            jax.ShapeDtypeStruct((N_TOK, RT_PAD), F32),
        ],
        scratch_shapes=[pltpu.VMEM((OUT_TM, D_MODEL), BF16)],
        compiler_params=pltpu.CompilerParams(
            dimension_semantics=("arbitrary",), vmem_limit_bytes=VMEM_LIMIT),
        name="out_proj",
    )(*y_ctx, *y_lat, x, mods, g, w_bf16, rt)


def _moe_kernel(blk_e_ref, n_used_ref, xb_ref, w13_ref, w2_ref, o_ref):
    i = pl.program_id(0)

    @pl.when(i < n_used_ref[0])
    def _():
        packed = xb_ref[...]
        hi = lax.bitcast_convert_type(packed & jnp.int32(-65536), F32)
        lo = lax.bitcast_convert_type(lax.shift_left(packed, 16), F32)
        xb = jnp.concatenate([hi, lo], -1).astype(BF16)
        gu = _dot(xb, w13_ref[0, 0])
        gt = gu[:, :D_EXPERT]
        up = gu[:, D_EXPERT:]
        a = (gt * _sigmoid(gt) * up).astype(BF16)
        o_ref[...] = _dot(a, w2_ref[0, 0])

    @pl.when(i >= n_used_ref[0])
    def _():
        o_ref[...] = jnp.zeros_like(o_ref)


def _moe_blocks(blk_e, n_used, xb, layer, w13_bf16, w2_bf16):
    grid_spec = pltpu.PrefetchScalarGridSpec(
        num_scalar_prefetch=2,
        grid=(MOE_NBLK,),
        in_specs=[
            pl.BlockSpec((MOE_BM, D_MODEL // 2), lambda i, be, nu: (i, 0)),
            pl.BlockSpec((1, 1, D_MODEL, 2 * D_EXPERT), lambda i, be, nu: (layer, be[i], 0, 0)),
            pl.BlockSpec((1, 1, D_EXPERT, D_MODEL), lambda i, be, nu: (layer, be[i], 0, 0)),
        ],
        out_specs=pl.BlockSpec((MOE_BM, D_MODEL), lambda i, be, nu: (i, 0)),
    )
    return pl.pallas_call(
        _moe_kernel,
        grid_spec=grid_spec,
        out_shape=jax.ShapeDtypeStruct((MOE_NBLK * MOE_BM, D_MODEL), F32),
        compiler_params=pltpu.CompilerParams(
            dimension_semantics=("arbitrary",), vmem_limit_bytes=VMEM_LIMIT),
        name="moe_blocks",
    )(blk_e, n_used, xb, w13_bf16, w2_bf16)


def _hmoe(h_bf16, logits, layer, w13_bf16, w2_bf16):
    n = N_TOK
    lg = logits[:, :N_GROUPS]
    grp = jnp.argmax(lg, -1).astype(jnp.int32)
    gate_g = jnp.take_along_axis(jax.nn.softmax(lg, -1), grp[:, None], -1)
    le = logits[:, N_GROUPS:N_GROUPS + N_EXPERTS].reshape(n, N_GROUPS, EXP_PER_GROUP)
    le = jnp.take_along_axis(le, grp[:, None, None], 1)[:, 0]
    top_l, top_i = lax.top_k(le, TOP_K)
    w = gate_g * jax.nn.softmax(top_l, -1)
    eid = grp[:, None] * EXP_PER_GROUP + top_i.astype(jnp.int32)

    a_tot = n * TOP_K
    rows = MOE_NBLK * MOE_BM
    flat_e = eid.reshape(-1)
    flat_t = jnp.arange(a_tot, dtype=jnp.int32) // TOP_K
    onehot = (flat_e[:, None] == jnp.arange(N_EXPERTS, dtype=jnp.int32)[None]).astype(jnp.int32)
    csum = jnp.cumsum(onehot, axis=0)
    counts = csum[-1]
    rank = jnp.take_along_axis(csum, flat_e[:, None], 1)[:, 0] - 1
    pcounts = (counts + MOE_BM - 1) // MOE_BM * MOE_BM
    pends = jnp.cumsum(pcounts)
    pstarts = pends - pcounts
    dest = pstarts[flat_e] + rank
    row_tok = jnp.zeros((rows,), jnp.int32).at[dest].set(flat_t)
    blk_start = jnp.arange(MOE_NBLK, dtype=jnp.int32) * MOE_BM
    n_used = (pends[-1] // MOE_BM).astype(jnp.int32)
    blk_e = jnp.searchsorted(pends, blk_start, side='right').astype(jnp.int32)
    last_e = jnp.searchsorted(pends, pends[-1] - 1, side='right').astype(jnp.int32)
    blk_e = jnp.minimum(blk_e, last_e)
    xb = h_bf16.at[row_tok].get(mode="promise_in_bounds")
    yb = _moe_blocks(blk_e, n_used.reshape(1), xb, layer, w13_bf16, w2_bf16)
    d2 = dest.reshape(n, TOP_K)
    take = lambda idx: yb.at[idx].get(mode="promise_in_bounds")
    return take(d2[:, 0]), take(d2[:, 1]), w


CMB_TM = 256


def _combine_kernel(final, x_ref, y0_ref, y1_ref, w_ref, mods_ref, nmods_ref, g_ref, *o_refs):
    moe = y0_ref[...] * w_ref[:, 0:1] + y1_ref[...] * w_ref[:, 1:2]
    x = x_ref[...] + mods_ref[0, 5:6, :] * moe
    if final:
        o_refs[0][...] = x * lax.rsqrt(jnp.mean(x * x, -1, keepdims=True) + NORM_EPS) * g_ref[...]
    else:
        o_refs[0][...] = x
        o_refs[1][...] = _modulated_norm(x, g_ref[...], nmods_ref[0, 0:1, :], nmods_ref[0, 1:2, :]).astype(BF16)


def _combine(final, x_new, y0, y1, w, mods, next_mods, g):
    per_seg = SEG // CMB_TM
    tok = pl.BlockSpec((CMB_TM, D_MODEL), lambda i: (i, 0))
    seg = pl.BlockSpec((1, 6, D_MODEL), lambda i: (i // per_seg, 0, 0))
    out_shape = [jax.ShapeDtypeStruct((N_TOK, D_MODEL), F32)]
    if not final:
        out_shape.append(jax.ShapeDtypeStruct((N_TOK, D_MODEL), BF16))
    return pl.pallas_call(
        functools.partial(_combine_kernel, final),
        grid=(N_TOK // CMB_TM,),
        in_specs=[tok, tok, tok, pl.BlockSpec((CMB_TM, TOP_K), lambda i: (i, 0)), seg, seg,
                  pl.BlockSpec((1, D_MODEL), lambda i: (0, 0))],
        out_specs=[tok] * len(out_shape),
        out_shape=out_shape,
        compiler_params=pltpu.CompilerParams(
            dimension_semantics=("arbitrary",), vmem_limit_bytes=VMEM_LIMIT),
        name="moe_combine",
    )(x_new, y0, y1, w, mods, next_mods, g)


def kernel(x_prompt, x_sample, cache_k, cache_v, state_rwkv, c, c_ctx, norm1_g, w_mod, b_mod, w_in,
           rw_shift, rw_w0, rw_w2, rw_a0, rw_a2, rw_g2, rw_kk, rw_ka, rw_rk, rw_ln_g, rw_ln_b, na_rpb,
           cv_dw, cv_dw_b, cv_ln_g, cv_ln_b, w_out, norm2_g, rt_group, rt_expert, ex_w13, ex_w2, final_g):
    x = jnp.concatenate([x_prompt.reshape(N_CTX_TOK, D_MODEL), x_sample.reshape(N_LAT_TOK, D_MODEL)], 0)
    cond = jnp.concatenate([c_ctx[None], c], 0)
    seg_cond = np.concatenate([np.zeros(N_CTX_SEG, np.int32), 1 + np.arange(DEC_BATCH, dtype=np.int32)])
    zero_state = jnp.zeros((BATCH, N_PAIR, LANES, LANES), F32)
    w_in16 = w_in.astype(BF16)
    w_rw16, w_na16, w_cv16 = w_in16[..., :RW_IN], w_in16[..., RW_IN:RW_IN + NA_IN], w_in16[..., RW_IN + NA_IN:]
    w_out16 = w_out.astype(BF16)
    w13_16 = ex_w13.astype(BF16)
    w2_16 = ex_w2.astype(BF16)
    cond_pad = jnp.concatenate([cond, jnp.zeros((MOD_ROWS - 1 - DEC_BATCH, D_MODEL), F32)], 0)
    m_all = _adaln_mods(cond_pad, w_mod, b_mod[:, None, :])
    mods_all = [m_all[l].reshape(MOD_ROWS, 6, D_MODEL)[seg_cond] for l in range(DEPTH)]
    h = _norm_mod(x, mods_all[0], norm1_g[0][None])
    kc_all = cache_k.reshape(DEC_BATCH, DEPTH, PAST_LEN, C_NA)
    vc_all = cache_v.reshape(DEC_BATCH, DEPTH, PAST_LEN, C_NA)
    bias_all = _natten_bias(na_rpb)
    ks_out, vs_out, ss_out = [], [], []
    for l in range(DEPTH):
        W = dict(rw_shift=rw_shift[l], rw_w0=rw_w0[l], rw_w2=rw_w2[l], rw_a0=rw_a0[l], rw_a2=rw_a2[l],
                 rw_g2=rw_g2[l], rw_kk=rw_kk[l], rw_ka=rw_ka[l], rw_rk=rw_rk[l], rw_ln_g=rw_ln_g[l],
                 rw_ln_b=rw_ln_b[l], cv_dw=cv_dw[l], cv_dw_b=cv_dw_b[l], cv_ln_g=cv_ln_g[l],
                 cv_ln_b=cv_ln_b[l])
        mods = mods_all[l]
        p_rw, p_na, p_cv = _project(h, l, w_rw16, w_na16, w_cv16)
        y_rw_c, s_f, s_b = _rwkv_mix(p_rw, zero_state, zero_state, BATCH, SEQ, 0, W)
        y_na_c = _attention(p_na, BATCH, SEQ, 0)
        y_cv_c = _conv_module(p_cv, BATCH, SEQ, 0, W)
        s0 = state_rwkv[:, l]
        y_rw_l, _, _ = _rwkv_mix(p_rw, _state_to_tiles(s0[:, 0]), _state_to_tiles(s0[:, 1]),
                                 DEC_BATCH, DEC_SEQ, N_CTX_TOK, W)
        y_na_l = _attention(p_na, DEC_BATCH, DEC_SEQ, N_CTX_TOK, l, kc_all, vc_all, bias_all)
        y_cv_l = _conv_module(p_cv, DEC_BATCH, DEC_SEQ, N_CTX_TOK, W)
        ks_out.append(p_na[:N_CTX_TOK, C_NA:2 * C_NA].reshape(BATCH, SEQ, H_NA, HD))
        vs_out.append(p_na[:N_CTX_TOK, 2 * C_NA:].reshape(BATCH, SEQ, H_NA, HD))
        ss_out.append(jnp.stack([_tiles_to_state(s_f), _tiles_to_state(s_b)], 1))
        rt = jnp.concatenate([rt_group[l], rt_expert[l],
                              jnp.zeros((D_MODEL, RT_PAD - N_GROUPS - N_EXPERTS), F32)], 1)
        x_new, h2, logits = _out_project((y_rw_c, y_na_c, y_cv_c), (y_rw_l, y_na_l, y_cv_l), x, mods,
                                         norm2_g[l][None], l, w_out16, rt)
        y0, y1, w_tok = _hmoe(h2, logits, l, w13_16, w2_16)
        if l == DEPTH - 1:
            (y,) = _combine(True, x_new, y0, y1, w_tok, mods, mods, final_g[None])
        else:
            x, h = _combine(False, x_new, y0, y1, w_tok, mods, mods_all[l + 1], norm1_g[l + 1][None])
    y_prompt = y[:N_CTX_TOK].reshape(BATCH, SEQ, D_MODEL)
    y_sample = y[N_CTX_TOK:].reshape(DEC_BATCH, DEC_SEQ, D_MODEL)
    return (y_prompt, y_sample, jnp.stack(ks_out, axis=1), jnp.stack(vs_out, axis=1),
            jnp.stack(ss_out, axis=1))
```

```python
import functools

import jax
import jax.numpy as jnp
import numpy as np
from jax import lax
from jax.experimental import pallas as pl
from jax.experimental.pallas import tpu as pltpu

D_MODEL = 2048
BATCH = 16
SEQ = 256
DEPTH = 4
DEC_BATCH = 8
DEC_SEQ = 1024
PAST_LEN = 512

GRID_W = 64
HD = 64
H_RW = 12
C_RW = H_RW * HD
H_NA = 12
C_NA = H_NA * HD
C_CV = D_MODEL - C_RW - C_NA
R_W = 64
R_A = 64
R_G = 128
RW_IN = 3 * C_RW + 2 * R_W + 2 * R_A + R_G
NA_IN = 3 * C_NA
CV_IN = 2 * C_CV
NA_KH_MAX = 8
NA_KW = 16
CONV_K = 31
N_GROUPS = 4
EXP_PER_GROUP = 8
N_EXPERTS = N_GROUPS * EXP_PER_GROUP
TOP_K = 2
D_EXPERT = 1024
NORM_EPS = 1e-6
LN_EPS = 1e-5
RW_GN_EPS = 64e-5
NEG = -1e30

LANES = 128
N_PAIR = H_RW // 2
SEG = 1024
N_CTX_TOK = BATCH * SEQ
N_LAT_TOK = DEC_BATCH * DEC_SEQ
N_TOK = N_CTX_TOK + N_LAT_TOK
N_SEG = N_TOK // SEG
N_CTX_SEG = N_CTX_TOK // SEG
RT_PAD = LANES
MOE_BM = 256
MOE_NBLK = (N_TOK * TOP_K) // MOE_BM + N_EXPERTS
VMEM_LIMIT = 48 * 1024 * 1024

BF16 = jnp.bfloat16
F32 = jnp.float32


def _dot(a, b):
    return jnp.dot(a, b, preferred_element_type=F32)


def _dot_t(a, b):
    return lax.dot_general(a, b, (((1,), (1,)), ((), ())), preferred_element_type=F32)


def _pack_bf16_pairs(x):
    n = x.shape[-1] // 2
    bits = lax.bitcast_convert_type(x.astype(BF16).astype(F32), jnp.int32)
    return bits[:, :n] | lax.shift_right_logical(bits[:, n:], 16)


def _unpack_bf16_pairs(packed):
    hi = lax.bitcast_convert_type(packed & jnp.int32(-65536), F32)
    lo = lax.bitcast_convert_type(lax.shift_left(packed, 16), F32)
    return jnp.concatenate([hi, lo], -1)


def _split_bf16_resid(x):
    hi = x.astype(BF16)
    return hi, x - hi.astype(F32)


def _sigmoid(x):
    return 1.0 / (1.0 + jnp.exp(-x))


def _head_sum(x, m_a):
    s_a = jnp.sum(jnp.where(m_a, x, 0.0), -1, keepdims=True)
    s_b = jnp.sum(jnp.where(m_a, 0.0, x), -1, keepdims=True)
    return jnp.where(m_a, s_a, s_b)


MOD_ROWS = 16
MOD_TN = 1024


def _mods_kernel(c_ref, w_ref, b_ref, o_ref):
    c = c_ref[...]
    o_ref[0] = _dot((c * _sigmoid(c)).astype(BF16), w_ref[0].astype(BF16)) + b_ref[0]


def _adaln_mods(cond, w_mod, b_mod):
    return pl.pallas_call(
        _mods_kernel,
        grid=(DEPTH, 6 * D_MODEL // MOD_TN),
        in_specs=[
            pl.BlockSpec((MOD_ROWS, D_MODEL), lambda l, j: (0, 0)),
            pl.BlockSpec((1, D_MODEL, MOD_TN), lambda l, j: (l, 0, j)),
            pl.BlockSpec((1, 1, MOD_TN), lambda l, j: (l, 0, j)),
        ],
        out_specs=pl.BlockSpec((1, MOD_ROWS, MOD_TN), lambda l, j: (l, 0, j)),
        out_shape=jax.ShapeDtypeStruct((DEPTH, MOD_ROWS, 6 * D_MODEL), F32),
        compiler_params=pltpu.CompilerParams(
            dimension_semantics=("arbitrary", "arbitrary"), vmem_limit_bytes=VMEM_LIMIT),
        name="adaln_mods",
    )(cond, w_mod, b_mod)


NORM_TM = 256


def _modulated_norm(x, g, shift, scale):
    y = x * lax.rsqrt(jnp.mean(x * x, -1, keepdims=True) + NORM_EPS)
    return (y * g) * (1.0 + scale) + shift


def _norm_kernel(x_ref, mods_ref, g_ref, h_ref):
    h_ref[...] = _modulated_norm(x_ref[...], g_ref[...], mods_ref[0, 0:1, :], mods_ref[0, 1:2, :]).astype(BF16)


def _norm_mod(x, mods, g):
    tok = pl.BlockSpec((NORM_TM, D_MODEL), lambda i: (i, 0))
    return pl.pallas_call(
        _norm_kernel,
        grid=(N_TOK // NORM_TM,),
        in_specs=[tok, pl.BlockSpec((1, 6, D_MODEL), lambda i: (i // (SEG // NORM_TM), 0, 0)),
                  pl.BlockSpec((1, D_MODEL), lambda i: (0, 0))],
        out_specs=tok,
        out_shape=jax.ShapeDtypeStruct((N_TOK, D_MODEL), BF16),
        compiler_params=pltpu.CompilerParams(
            dimension_semantics=("arbitrary",), vmem_limit_bytes=VMEM_LIMIT),
        name="norm_mod",
    )(x, mods, g)


PROJ_TM = SEG
PROJ_RW_TN = RW_IN // 3
PROJ_NA_TN = NA_IN // 3
PROJ_CV_TN = CV_IN // 2
PROJ_STEPS = 8


def _proj_kernel(h_ref, wrw_ref, wna_ref, wcv_ref, orw_ref, ona_ref, ocv_ref):
    j = pl.program_id(1)

    @pl.when(j < 3)
    def _():
        orw_ref[0] = _dot(h_ref[...], wrw_ref[0, 0])

    @pl.when((j >= 3) & (j < 6))
    def _():
        ona_ref[0] = _dot(h_ref[...], wna_ref[0, 0])

    @pl.when(j >= 6)
    def _():
        ocv_ref[0] = _dot(h_ref[...], wcv_ref[0, 0])


def _project(h, layer, w_rw, w_na, w_cv):
    def c_rw(j):
        return jnp.minimum(j, 2)

    def c_na(j):
        return jnp.clip(j - 3, 0, 2)

    def c_cv(j):
        return jnp.clip(j - 6, 0, 1)

    return pl.pallas_call(
        _proj_kernel,
        grid=(N_TOK // PROJ_TM, PROJ_STEPS),
        in_specs=[
            pl.BlockSpec((PROJ_TM, D_MODEL), lambda i, j: (i, 0)),
            pl.BlockSpec((1, 1, D_MODEL, PROJ_RW_TN), lambda i, j: (layer, c_rw(j), 0, 0)),
            pl.BlockSpec((1, 1, D_MODEL, PROJ_NA_TN), lambda i, j: (layer, c_na(j), 0, 0)),
            pl.BlockSpec((1, 1, D_MODEL, PROJ_CV_TN), lambda i, j: (layer, c_cv(j), 0, 0)),
        ],
        out_specs=[
            pl.BlockSpec((1, PROJ_TM, PROJ_RW_TN), lambda i, j: (c_rw(j), i, 0)),
            pl.BlockSpec((1, PROJ_TM, PROJ_NA_TN), lambda i, j: (c_na(j), i, 0)),
            pl.BlockSpec((1, PROJ_TM, PROJ_CV_TN), lambda i, j: (c_cv(j), i, 0)),
        ],
        out_shape=[
            jax.ShapeDtypeStruct((RW_IN // PROJ_RW_TN, N_TOK, PROJ_RW_TN), F32),
            jax.ShapeDtypeStruct((NA_IN // PROJ_NA_TN, N_TOK, PROJ_NA_TN), F32),
            jax.ShapeDtypeStruct((CV_IN // PROJ_CV_TN, N_TOK, PROJ_CV_TN), F32),
        ],
        compiler_params=pltpu.CompilerParams(
            dimension_semantics=("arbitrary", "arbitrary"), vmem_limit_bytes=VMEM_LIMIT),
        name="in_proj",
    )(h, w_rw, w_na, w_cv)


RW_C = 64
RW_NCH = 4
RW_BLK = RW_NCH * RW_C
RW_HALO = 8
RW_SB = 16
RW_SB_SQUARINGS = 3
assert RW_C == 4 * RW_SB


def _rwkv_kernel(reverse, final, n_blocks, *refs):
    if final:
        (p_ref, pp_ref, pn_ref, s0_ref, shift_ref, w0_ref, w2_ref, a0_ref, a2_ref, ka_ref, kkw_ref, rk_ref,
         tri_ref, g2_ref, lng_ref, lnb_ref, ysin_ref, bnin_ref, y_ref, sfin_ref, s_scr) = refs
    else:
        (p_ref, pp_ref, pn_ref, s0_ref, shift_ref, w0_ref, w2_ref, a0_ref, a2_ref, ka_ref, kkw_ref, rk_ref,
         tri_ref, ys_ref, bn_ref, sfin_ref, s_scr) = refs
    c = pl.program_id(1)
    cc = (n_blocks - 1 - c) if reverse else c

    @pl.when(c == 0)
    def _():
        s_scr[...] = s0_ref[0]

    n_cb = RW_IN // PROJ_RW_TN
    x = jnp.concatenate([p_ref[cb] for cb in range(n_cb)], -1)
    row = lax.broadcasted_iota(jnp.int32, (RW_BLK, 1), 0)
    hp = jnp.concatenate([pp_ref[cb, RW_HALO - 1:RW_HALO, :] for cb in range(n_cb)], -1)
    hn = jnp.concatenate([pn_ref[cb, 0:1, :] for cb in range(n_cb)], -1)
    hp = jnp.where(cc > 0, hp, 0.0)
    hn = jnp.where(cc < n_blocks - 1, hn, 0.0)
    prev = jnp.where(row == 0, hp, pltpu.roll(x, 1, 0))
    nxt = jnp.where(row == RW_BLK - 1, hn, pltpu.roll(x, RW_BLK - 1, 0))
    xs = x + shift_ref[0:1, :] * (prev - x) + shift_ref[1:2, :] * (nxt - x)

    r = xs[:, 0:C_RW]
    k = xs[:, C_RW:2 * C_RW]
    v = xs[:, 2 * C_RW:3 * C_RW]
    wd = xs[:, 3 * C_RW:3 * C_RW + 2 * R_W]
    ad = xs[:, 3 * C_RW + 2 * R_W:3 * C_RW + 2 * R_W + 2 * R_A]
    wl = w0_ref[...] + _dot(jnp.tanh(wd).astype(BF16), w2_ref[...])
    softplus_neg = jnp.maximum(-wl, 0.0) + jnp.log(1.0 + jnp.exp(-jnp.abs(wl)))
    lw = -jnp.exp(-softplus_neg - 0.5)
    a_sig = _sigmoid(a0_ref[...] + _dot(ad.astype(BF16), a2_ref[...]))
    kmod = k * (1.0 + (a_sig - 1.0) * ka_ref[...])
    kkf = k * kkw_ref[...]
    bonus_w = r * kmod * rk_ref[...]
    tri = tri_ref[...]
    lw_hi, lw_r = _split_bf16_resid(lw)
    lw_mid, lw_r = _split_bf16_resid(lw_r)
    cl = _dot(tri, lw_hi) + (_dot(tri, lw_mid) + _dot(tri, lw_r.astype(BF16)))
    end_rows = [j * RW_C if reverse else (j + 1) * RW_C - 1 for j in range(RW_NCH)]
    cl_ends = [cl[e:e + 1, :] for e in end_rows]
    cl_end = jnp.concatenate([jnp.broadcast_to(ce, (RW_C, C_RW)) for ce in cl_ends], 0)
    g_in = jnp.exp(cl)
    g_inv = jnp.exp(-cl)
    g_prev = jnp.exp(cl - lw)
    g_end = jnp.exp(cl_end - cl)
    g_tot = [jnp.exp(ce) for ce in cl_ends]

    lane = lax.broadcasted_iota(jnp.int32, (1, LANES), 1)
    m_a = lane < HD
    ri = lax.broadcasted_iota(jnp.int32, (2 * RW_C, 2 * RW_C), 0)
    ci = lax.broadcasted_iota(jnp.int32, (2 * RW_C, 2 * RW_C), 1)
    strict = (ri < ci) if reverse else (ri > ci)
    incl = (ri <= ci) if reverse else (ri >= ci)
    eye = ri == ci
    sub_blk = (ri // RW_SB) == (ci // RW_SB)

    def stack(t):
        return jnp.concatenate([jnp.where(m_a, t, 0.0), jnp.where(m_a, 0.0, t)], 0)

    if final:
        gd = xs[:, 3 * C_RW + 2 * R_W + 2 * R_A:]
        gate = _dot(_sigmoid(gd).astype(BF16), g2_ref[...])

    n = 2 * RW_C
    sls = [slice(p * LANES, (p + 1) * LANES) for p in range(N_PAIR)]
    rws = [slice(j * RW_C, (j + 1) * RW_C) for j in range(RW_NCH)]
    items = [(j, p) for j in range(RW_NCH) for p in range(N_PAIR)]
    pairs = range(len(items))
    z_a, z_r, z_v, z_bh, z_kh, bonus = [], [], [], [], [], []
    l_ab, l_ak, t_rb, t_rk = [], [], [], []
    for j, p in items:
        rs, sl = rws[j], sls[p]
        r_p, k_p, v_p, a_p = r[rs, sl], kmod[rs, sl], v[rs, sl], a_sig[rs, sl]
        kkf_p = kkf[rs, sl]
        kk = kkf_p * lax.rsqrt(jnp.maximum(_head_sum(kkf_p * kkf_p, m_a), 1e-24))
        bb = kk * a_p
        bonus.append(_head_sum(bonus_w[rs, sl], m_a) * v_p)
        z_a.append(stack(-kk * g_prev[rs, sl]))
        z_r.append(stack(r_p * g_in[rs, sl]))
        z_b = stack(bb * g_inv[rs, sl])
        z_k = stack(k_p * g_inv[rs, sl])
        z_bh.append(stack(bb * g_end[rs, sl]))
        z_kh.append(stack(k_p * g_end[rs, sl]))
        z_v.append(stack(v_p))
        gram = _dot_t(jnp.concatenate([z_a[-1], z_r[-1]], 0).astype(BF16),
                      jnp.concatenate([z_b, z_k], 0).astype(BF16))
        l_ab.append(jnp.where(strict, gram[:n, :n], 0.0).astype(BF16))
        l_ak.append(jnp.where(strict, gram[:n, n:], 0.0).astype(BF16))
        t_rb.append(jnp.where(incl, gram[n:, :n], 0.0))
        t_rk.append(jnp.where(incl, gram[n:, n:], 0.0))
    xx = [jnp.concatenate([z_a[p], _dot(l_ak[p], z_v[p].astype(BF16))], 1) for p in pairs]
    pw = [jnp.where(sub_blk, l_ab[p], 0.0) for p in pairs]
    n_off = [jnp.where(sub_blk, 0.0, l_ab[p]) for p in pairs]
    td = [jnp.where(eye, 1.0, pw[p].astype(F32)) for p in pairs]
    for _ in range(RW_SB_SQUARINGS):
        pw = [_dot(pw[p], pw[p]).astype(BF16) for p in pairs]
        td = [td[p] + _dot(td[p].astype(BF16), pw[p]) for p in pairs]
    td = [td[p].astype(BF16) for p in pairs]
    mm = [_dot(td[p], n_off[p]).astype(BF16) for p in pairs]
    xx = [_dot(td[p], xx[p].astype(BF16)) for p in pairs]
    xx = [xx[p] + _dot(mm[p], xx[p].astype(BF16)) for p in pairs]
    mm = [_dot(mm[p], mm[p]).astype(BF16) for p in pairs]
    xx = [xx[p] + _dot(mm[p], xx[p].astype(BF16)) for p in pairs]
    res = []
    for p in pairs:
        lhs = jnp.concatenate([jnp.concatenate([t_rb[p], t_rk[p]], 1),
                               jnp.concatenate([z_bh[p].T, z_kh[p].T], 1)], 0).astype(BF16)
        rhs = jnp.concatenate([xx[p], jnp.concatenate([jnp.zeros((n, LANES), F32), z_v[p]], 1)], 0).astype(BF16)
        res.append(_dot(lhs, rhs))
    state = [s_scr[p] for p in range(N_PAIR)]
    for j in (reversed(range(RW_NCH)) if reverse else range(RW_NCH)):
        for p in range(N_PAIR):
            i, rs, sl = j * N_PAIR + p, rws[j], sls[p]
            q_m = z_r[i] + res[i][:n, :LANES]
            g_m = jnp.where(eye, g_tot[j][:, sl], 0.0) + res[i][n:, :LANES]
            upd = _dot(jnp.concatenate([q_m, g_m], 0).astype(BF16), state[p].astype(BF16))
            ys = upd[:n] + res[i][:n, LANES:]
            state[p] = upd[n:] + res[i][n:, LANES:]
            y_p = ys[:RW_C] + ys[RW_C:]
            if final:
                y_t = y_p + ysin_ref[rs, sl]
                mu = _head_sum(y_t, m_a) * (1.0 / HD)
                d = y_t - mu
                var = _head_sum(d * d, m_a) * (1.0 / HD)
                y_n = d * lax.rsqrt(var + RW_GN_EPS) * lng_ref[:, sl] + lnb_ref[:, sl]
                y_ref[rs, sl] = ((y_n + bonus[i] + bnin_ref[rs, sl]) * gate[rs, sl]).astype(y_ref.dtype)
            else:
                ys_ref[rs, sl] = y_p
                bn_ref[rs, sl] = bonus[i]
    for p in range(N_PAIR):
        s_scr[p] = state[p]

    @pl.when(c == n_blocks - 1)
    def _():
        sfin_ref[0] = s_scr[...]


def _rwkv_pass(reverse, final, batch, seq, base_row, p_rw, s0, wts, extra):
    n_blocks = seq // RW_BLK
    n_cb = RW_IN // PROJ_RW_TN
    base_blk = base_row // RW_BLK
    halo_per_chunk = RW_BLK // RW_HALO
    n_halo_blk = N_TOK // RW_HALO

    def chunk(c):
        return (n_blocks - 1 - c) if reverse else c

    def row_blk(b, c):
        return base_blk + b * n_blocks + chunk(c)

    def prev_blk(b, c):
        return jnp.maximum(row_blk(b, c) * halo_per_chunk - 1, 0)

    def next_blk(b, c):
        return jnp.minimum((row_blk(b, c) + 1) * halo_per_chunk, n_halo_blk - 1)

    full = lambda shape: pl.BlockSpec(shape, lambda b, c: (0,) * len(shape))
    tok = lambda width: pl.BlockSpec((RW_BLK, width), lambda b, c: (row_blk(b, c), 0))
    loc = lambda width: pl.BlockSpec((RW_BLK, width), lambda b, c: (b * n_blocks + chunk(c), 0))
    state = pl.BlockSpec((1, N_PAIR, LANES, LANES), lambda b, c: (b, 0, 0, 0))
    in_specs = [
        pl.BlockSpec((n_cb, RW_BLK, PROJ_RW_TN), lambda b, c: (0, row_blk(b, c), 0)),
        pl.BlockSpec((n_cb, RW_HALO, PROJ_RW_TN), lambda b, c: (0, prev_blk(b, c), 0)),
        pl.BlockSpec((n_cb, RW_HALO, PROJ_RW_TN), lambda b, c: (0, next_blk(b, c), 0)),
        state,
        full((2, RW_IN)), full((1, C_RW)), full((2 * R_W, C_RW)), full((1, C_RW)), full((2 * R_A, C_RW)),
        full((1, C_RW)), full((1, C_RW)), full((1, C_RW)), full((RW_BLK, RW_BLK)),
    ]
    args = [p_rw, p_rw, p_rw, s0] + list(wts)
    if final:
        in_specs += [full((R_G, C_RW)), full((1, C_RW)), full((1, C_RW)), loc(C_RW), loc(C_RW)]
        args += list(extra)
        out_specs = [loc(C_RW), state]
        out_shape = [jax.ShapeDtypeStruct((batch * seq, C_RW), BF16),
                     jax.ShapeDtypeStruct((batch, N_PAIR, LANES, LANES), F32)]
    else:
        out_specs = [loc(C_RW), loc(C_RW), state]
        out_shape = [jax.ShapeDtypeStruct((batch * seq, C_RW), F32), jax.ShapeDtypeStruct((batch * seq, C_RW), F32),
                     jax.ShapeDtypeStruct((batch, N_PAIR, LANES, LANES), F32)]
    return pl.pallas_call(
        functools.partial(_rwkv_kernel, reverse, final, n_blocks),
        grid=(batch, n_blocks),
        in_specs=in_specs,
        out_specs=out_specs,
        out_shape=out_shape,
        scratch_shapes=[pltpu.VMEM((N_PAIR, LANES, LANES), F32)],
        compiler_params=pltpu.CompilerParams(
            dimension_semantics=("arbitrary", "arbitrary"), vmem_limit_bytes=VMEM_LIMIT),
        name="rwkv_bwd" if reverse else "rwkv_fwd",
    )(*args)


def _state_to_tiles(s):
    b = s.shape[0]
    st = jnp.swapaxes(s, -1, -2).reshape(b, N_PAIR, 2, HD, HD)
    z = jnp.zeros((b, N_PAIR, HD, HD), s.dtype)
    top = jnp.concatenate([st[:, :, 0], z], -1)
    bot = jnp.concatenate([z, st[:, :, 1]], -1)
    return jnp.concatenate([top, bot], -2)


def _tiles_to_state(t):
    b = t.shape[0]
    s_a = t[:, :, :HD, :HD]
    s_b = t[:, :, HD:, HD:]
    st = jnp.stack([s_a, s_b], 2).reshape(b, H_RW, HD, HD)
    return jnp.swapaxes(st, -1, -2)


def _rwkv_mix(p_rw, s0_fwd, s0_bwd, batch, seq, base_row, W):
    def dir_weights(d):
        pad = lambda m: jnp.zeros((2 * m.shape[1], C_RW), BF16).at[d * m.shape[1]:(d + 1) * m.shape[1]].set(
            m[d].astype(BF16))
        tri = np.triu(np.ones((RW_C, RW_C), np.float32)) if d == 1 else np.tril(np.ones((RW_C, RW_C), np.float32))
        tri = np.kron(np.eye(RW_NCH, dtype=np.float32), tri)
        return [W['rw_shift'], W['rw_w0'][d][None], pad(W['rw_w2']), W['rw_a0'][d][None], pad(W['rw_a2']),
                W['rw_ka'][None], W['rw_kk'][None], W['rw_rk'].reshape(1, C_RW), jnp.asarray(tri, BF16)]

    ys_b, bn_b, s_b = _rwkv_pass(True, False, batch, seq, base_row, p_rw, s0_bwd, dir_weights(1), None)
    extra = [W['rw_g2'].astype(BF16), W['rw_ln_g'][None], W['rw_ln_b'][None], ys_b, bn_b]
    y, s_f = _rwkv_pass(False, True, batch, seq, base_row, p_rw, s0_fwd, dir_weights(0), extra)
    return y, s_f, s_b


ATT_TQ = 256


def _window_rows(i):
    rows = DEC_SEQ // GRID_W
    kh = min(NA_KH_MAX, rows)
    return min(max(i - kh // 2, 0), rows - kh), kh


def _window_keys(t):
    q_rows = range(t * ATT_TQ // GRID_W, (t + 1) * ATT_TQ // GRID_W)
    lo = min(_window_rows(i)[0] for i in q_rows)
    hi = max(sum(_window_rows(i)) for i in q_rows)
    per_tile = LANES // GRID_W
    lo, hi = lo // per_tile * per_tile, -(-hi // per_tile) * per_tile
    return slice(lo * GRID_W, hi * GRID_W)


def _attn_kernel(n_q, has_ctx, *refs):
    if has_ctx:
        q_ref, k_ref, v_ref, kc_ref, vc_ref, bias_ref, o_ref = refs
    else:
        q_ref, k_ref, v_ref, o_ref = refs
    lane = lax.broadcasted_iota(jnp.int32, (1, LANES), 1)
    m_a = lane < HD
    k = k_ref[0].astype(BF16)
    v = v_ref[0].astype(BF16)
    if has_ctx:
        kc = kc_ref[0, 0].astype(BF16)
        vc = vc_ref[0, 0].astype(BF16)
    for t in range(n_q // ATT_TQ):
        rows = slice(t * ATT_TQ, (t + 1) * ATT_TQ)
        q = q_ref[0, rows, :] * (HD ** -0.5)
        keys = _window_keys(t) if has_ctx else slice(0, n_q)
        k_t, v_t = k[keys, :], v[keys, :]
        outs = []
        for h in range(2):
            q_h = jnp.where(m_a if h == 0 else jnp.logical_not(m_a), q, 0.0).astype(BF16)
            s = _dot_t(q_h, k_t)
            if has_ctx:
                s = s + bias_ref[0, h, rows, keys]
                s_c = _dot_t(q_h, kc)
                m = jnp.maximum(jnp.max(s, -1, keepdims=True), jnp.max(s_c, -1, keepdims=True))
                e_c = jnp.exp(s_c - m)
                e = jnp.exp(s - m)
                den = jnp.sum(e, -1, keepdims=True) + jnp.sum(e_c, -1, keepdims=True)
                o = _dot(e.astype(BF16), v_t) + _dot(e_c.astype(BF16), vc)
            else:
                m = jnp.max(s, -1, keepdims=True)
                e = jnp.exp(s - m)
                den = jnp.sum(e, -1, keepdims=True)
                o = _dot(e.astype(BF16), v_t)
            outs.append(o / den)
        o_ref[rows, :] = jnp.where(m_a, outs[0], outs[1]).astype(o_ref.dtype)


def _attention(p_na, batch, seq, base_row, layer=None, kc=None, vc=None, bias=None):
    has_ctx = kc is not None
    base_blk = base_row // seq
    tok = lambda part: pl.BlockSpec((1, seq, LANES), lambda p, b: (part, base_blk + b, p))
    in_specs = [tok(0), tok(1), tok(2)]
    args = [p_na, p_na, p_na]
    if has_ctx:
        cache = pl.BlockSpec((1, 1, PAST_LEN, LANES), lambda p, b: (b, layer, 0, p))
        in_specs += [cache, cache, pl.BlockSpec((1, 2, seq, seq), lambda p, b: (layer, p, 0, 0))]
        args += [kc, vc, bias]
    return pl.pallas_call(
        functools.partial(_attn_kernel, seq, has_ctx),
        grid=(N_PAIR, batch),
        in_specs=in_specs,
        out_specs=pl.BlockSpec((seq, LANES), lambda p, b: (b, p)),
        out_shape=jax.ShapeDtypeStruct((batch * seq, C_NA), BF16),
        compiler_params=pltpu.CompilerParams(
            dimension_semantics=("arbitrary", "arbitrary"), vmem_limit_bytes=VMEM_LIMIT),
        name="natten_lat" if has_ctx else "attn_ctx",
    )(*args)


def _natten_bias(rpb):
    rows = DEC_SEQ // GRID_W
    kh = min(NA_KH_MAX, rows)
    qi = np.arange(rows)
    si = np.clip(qi - kh // 2, 0, rows - kh)
    ok_r = (qi[None, :] >= si[:, None]) & (qi[None, :] < si[:, None] + kh)
    idx_r = np.clip(qi[None, :] - qi[:, None] + NA_KH_MAX - 1, 0, 2 * NA_KH_MAX - 2)
    qj = np.arange(GRID_W)
    wj = np.clip(qj - NA_KW // 2, 0, GRID_W - NA_KW)
    ok_c = (qj[None, :] >= wj[:, None]) & (qj[None, :] < wj[:, None] + NA_KW)
    pad = GRID_W - NA_KW
    lead = rpb.shape[:-2]
    rp = jnp.pad(rpb, ((0, 0),) * (rpb.ndim - 1) + ((pad, pad),))
    toep = jnp.stack([rp[..., pad + NA_KW - 1 - j:pad + NA_KW - 1 - j + GRID_W] for j in range(GRID_W)], -2)
    toep = jnp.where(jnp.asarray(ok_c), toep, NEG)
    neg_blk = jnp.full(lead + (GRID_W, GRID_W), NEG, rpb.dtype)
    block_rows = [jnp.concatenate([toep[..., idx_r[i, k], :, :] if ok_r[i, k] else neg_blk for k in range(rows)], -1)
                  for i in range(rows)]
    return jnp.stack(block_rows, -3).reshape(lead + (DEC_SEQ, DEC_SEQ))


CV_PAD = 16
CV_TT = 256


def _conv_kernel(seq, val_ref, gate_ref, dw_ref, dwb_ref, lng_ref, lnb_ref, o_ref, u_scr, c_scr):
    zeros = jnp.zeros((CV_PAD, C_CV), F32)
    u_scr[0:CV_PAD, :] = zeros
    u_scr[seq + CV_PAD:seq + 2 * CV_PAD, :] = zeros
    u_scr[CV_PAD:seq + CV_PAD, :] = val_ref[0] * _sigmoid(gate_ref[0])
    off = CV_PAD - CONV_K // 2
    for cb in range(C_CV // LANES):
        cols = slice(cb * LANES, (cb + 1) * LANES)
        for t in range(seq // CV_TT):
            acc = jnp.zeros((CV_TT, LANES), F32)
            for j in range(CONV_K):
                acc = acc + dw_ref[j:j + 1, cols] * u_scr[t * CV_TT + off + j:t * CV_TT + off + j + CV_TT, cols]
            c_scr[t * CV_TT:(t + 1) * CV_TT, cols] = acc
    u = c_scr[...] + dwb_ref[...]
    mu = jnp.mean(u, -1, keepdims=True)
    d = u - mu
    var = jnp.mean(d * d, -1, keepdims=True)
    y = d * lax.rsqrt(var + LN_EPS) * lng_ref[...] + lnb_ref[...]
    o_ref[...] = (y * _sigmoid(y)).astype(o_ref.dtype)


def _conv_module(p_cv, batch, seq, base_row, W):
    base_blk = base_row // seq
    full = lambda shape: pl.BlockSpec(shape, lambda b: (0,) * len(shape))
    return pl.pallas_call(
        functools.partial(_conv_kernel, seq),
        grid=(batch,),
        in_specs=[
            pl.BlockSpec((1, seq, C_CV), lambda b: (0, base_blk + b, 0)),
            pl.BlockSpec((1, seq, C_CV), lambda b: (1, base_blk + b, 0)),
            full((CONV_K, C_CV)), full((1, C_CV)), full((1, C_CV)), full((1, C_CV)),
        ],
        out_specs=pl.BlockSpec((seq, C_CV), lambda b: (b, 0)),
        out_shape=jax.ShapeDtypeStruct((batch * seq, C_CV), BF16),
        scratch_shapes=[pltpu.VMEM((seq + 2 * CV_PAD, C_CV), F32), pltpu.VMEM((seq, C_CV), F32)],
        compiler_params=pltpu.CompilerParams(
            dimension_semantics=("arbitrary",), vmem_limit_bytes=VMEM_LIMIT),
        name="conv_module",
    )(p_cv, p_cv, W['cv_dw'], W['cv_dw_b'][None], W['cv_ln_g'][None], W['cv_ln_b'][None])


OUT_TM = 512
OUT_CTX_STEPS = N_CTX_TOK // OUT_TM


def _split_bf16(x):
    hi = x.astype(BF16)
    return hi, (x - hi.astype(F32)).astype(BF16)


def _out_kernel(yrw_c, yna_c, ycv_c, yrw_l, yna_l, ycv_l, x_ref, mods_ref, g_ref, w_ref, rt_ref,
                xo_ref, h_ref, lg_ref, y_scr):
    i = pl.program_id(0)

    @pl.when(i < OUT_CTX_STEPS)
    def _():
        y_scr[...] = jnp.concatenate([yrw_c[...], yna_c[...], ycv_c[...]], -1)

    @pl.when(i >= OUT_CTX_STEPS)
    def _():
        y_scr[...] = jnp.concatenate([yrw_l[...], yna_l[...], ycv_l[...]], -1)

    x_new = x_ref[...] + mods_ref[0, 2:3, :] * _dot(y_scr[...], w_ref[0])
    xo_ref[...] = x_new
    h = _modulated_norm(x_new, g_ref[...], mods_ref[0, 3:4, :], mods_ref[0, 4:5, :])
    h_hi, h_lo = _split_bf16(h)
    h_ref[...] = _pack_bf16_pairs(h)
    rt_hi, rt_lo = _split_bf16(rt_ref[...])
    lg_ref[...] = _dot(h_hi, rt_hi) + (_dot(h_hi, rt_lo) + _dot(h_lo, rt_hi))


def _out_project(y_ctx, y_lat, x, mods, g, layer, w_bf16, rt):
    per_seg = SEG // OUT_TM
    tok = lambda width: pl.BlockSpec((OUT_TM, width), lambda i: (i, 0))
    ctx = lambda width: pl.BlockSpec((OUT_TM, width), lambda i: (jnp.minimum(i, OUT_CTX_STEPS - 1), 0))
    lat = lambda width: pl.BlockSpec((OUT_TM, width), lambda i: (jnp.maximum(i - OUT_CTX_STEPS, 0), 0))
    return pl.pallas_call(
        _out_kernel,
        grid=(N_TOK // OUT_TM,),
        in_specs=[
            ctx(C_RW), ctx(C_NA), ctx(C_CV), lat(C_RW), lat(C_NA), lat(C_CV), tok(D_MODEL),
            pl.BlockSpec((1, 6, D_MODEL), lambda i: (i // per_seg, 0, 0)),
            pl.BlockSpec((1, D_MODEL), lambda i: (0, 0)),
            pl.BlockSpec((1, D_MODEL, D_MODEL), lambda i: (layer, 0, 0)),
            pl.BlockSpec((D_MODEL, RT_PAD), lambda i: (0, 0)),
        ],
        out_specs=[tok(D_MODEL), tok(D_MODEL // 2), tok(RT_PAD)],
        out_shape=[
            jax.ShapeDtypeStruct((N_TOK, D_MODEL), F32),
            jax.ShapeDtypeStruct((N_TOK, D_MODEL // 2), jnp.int32),
            jax.ShapeDtypeStruct((N_TOK, RT_PAD), F32),
        ],
        scratch_shapes=[pltpu.VMEM((OUT_TM, D_MODEL), BF16)],
        compiler_params=pltpu.CompilerParams(
            dimension_semantics=("arbitrary",), vmem_limit_bytes=VMEM_LIMIT),
        name="out_proj",
    )(*y_ctx, *y_lat, x, mods, g, w_bf16, rt)


def _moe_kernel(blk_e_ref, n_used_ref, xb_ref, w13_ref, w2_ref, o_ref):
    i = pl.program_id(0)

    @pl.when(i < n_used_ref[0])
    def _():
        xb = _unpack_bf16_pairs(xb_ref[...]).astype(BF16)
        gu = _dot(xb, w13_ref[0, 0])
        gt = gu[:, :D_EXPERT]
        up = gu[:, D_EXPERT:]
        a = (gt * _sigmoid(gt) * up).astype(BF16)
        o_ref[...] = _pack_bf16_pairs(_dot(a, w2_ref[0, 0]))

    @pl.when(i >= n_used_ref[0])
    def _():
        o_ref[...] = jnp.zeros_like(o_ref)


def _moe_blocks(blk_e, n_used, xb, layer, w13_bf16, w2_bf16):
    grid_spec = pltpu.PrefetchScalarGridSpec(
        num_scalar_prefetch=2,
        grid=(MOE_NBLK,),
        in_specs=[
            pl.BlockSpec((MOE_BM, D_MODEL // 2), lambda i, be, nu: (i, 0)),
            pl.BlockSpec((1, 1, D_MODEL, 2 * D_EXPERT), lambda i, be, nu: (layer, be[i], 0, 0)),
            pl.BlockSpec((1, 1, D_EXPERT, D_MODEL), lambda i, be, nu: (layer, be[i], 0, 0)),
        ],
        out_specs=pl.BlockSpec((MOE_BM, D_MODEL // 2), lambda i, be, nu: (i, 0)),
    )
    return pl.pallas_call(
        _moe_kernel,
        grid_spec=grid_spec,
        out_shape=jax.ShapeDtypeStruct((MOE_NBLK * MOE_BM, D_MODEL // 2), jnp.int32),
        compiler_params=pltpu.CompilerParams(
            dimension_semantics=("arbitrary",), vmem_limit_bytes=VMEM_LIMIT),
        name="moe_blocks",
    )(blk_e, n_used, xb, w13_bf16, w2_bf16)


def _hmoe(h_bf16, logits, layer, w13_bf16, w2_bf16):
    n = N_TOK
    lg = logits[:, :N_GROUPS]
    grp = jnp.argmax(lg, -1).astype(jnp.int32)
    gate_g = jnp.take_along_axis(jax.nn.softmax(lg, -1), grp[:, None], -1)
    le = logits[:, N_GROUPS:N_GROUPS + N_EXPERTS].reshape(n, N_GROUPS, EXP_PER_GROUP)
    le = jnp.take_along_axis(le, grp[:, None, None], 1)[:, 0]
    top_l, top_i = lax.top_k(le, TOP_K)
    w = gate_g * jax.nn.softmax(top_l, -1)
    eid = grp[:, None] * EXP_PER_GROUP + top_i.astype(jnp.int32)

    a_tot = n * TOP_K
    rows = MOE_NBLK * MOE_BM
    flat_e = eid.reshape(-1)
    flat_t = jnp.arange(a_tot, dtype=jnp.int32) // TOP_K
    onehot = (flat_e[:, None] == jnp.arange(N_EXPERTS, dtype=jnp.int32)[None]).astype(jnp.int32)
    csum = jnp.cumsum(onehot, axis=0)
    counts = csum[-1]
    rank = jnp.take_along_axis(csum, flat_e[:, None], 1)[:, 0] - 1
    pcounts = (counts + MOE_BM - 1) // MOE_BM * MOE_BM
    pends = jnp.cumsum(pcounts)
    pstarts = pends - pcounts
    dest = pstarts[flat_e] + rank
    row_tok = jnp.zeros((rows,), jnp.int32).at[dest].set(flat_t)
    blk_start = jnp.arange(MOE_NBLK, dtype=jnp.int32) * MOE_BM
    n_used = (pends[-1] // MOE_BM).astype(jnp.int32)
    blk_e = jnp.searchsorted(pends, blk_start, side='right').astype(jnp.int32)
    last_e = jnp.searchsorted(pends, pends[-1] - 1, side='right').astype(jnp.int32)
    blk_e = jnp.minimum(blk_e, last_e)
    xb = h_bf16.at[row_tok].get(mode="promise_in_bounds")
    yb = _moe_blocks(blk_e, n_used.reshape(1), xb, layer, w13_bf16, w2_bf16)
    d2 = dest.reshape(n, TOP_K)
    take = lambda idx: yb.at[idx].get(mode="promise_in_bounds")
    return take(d2[:, 0]), take(d2[:, 1]), w


CMB_TM = 256


def _combine_kernel(final, x_ref, y0_ref, y1_ref, w_ref, mods_ref, nmods_ref, g_ref, *o_refs):
    moe = _unpack_bf16_pairs(y0_ref[...]) * w_ref[:, 0:1] + _unpack_bf16_pairs(y1_ref[...]) * w_ref[:, 1:2]
    x = x_ref[...] + mods_ref[0, 5:6, :] * moe
    if final:
        o_refs[0][...] = x * lax.rsqrt(jnp.mean(x * x, -1, keepdims=True) + NORM_EPS) * g_ref[...]
    else:
        o_refs[0][...] = x
        o_refs[1][...] = _modulated_norm(x, g_ref[...], nmods_ref[0, 0:1, :], nmods_ref[0, 1:2, :]).astype(BF16)


def _combine(final, x_new, y0, y1, w, mods, next_mods, g):
    per_seg = SEG // CMB_TM
    tok = pl.BlockSpec((CMB_TM, D_MODEL), lambda i: (i, 0))
    seg = pl.BlockSpec((1, 6, D_MODEL), lambda i: (i // per_seg, 0, 0))
    half = pl.BlockSpec((CMB_TM, D_MODEL // 2), lambda i: (i, 0))
    out_shape = [jax.ShapeDtypeStruct((N_TOK, D_MODEL), F32)]
    if not final:
        out_shape.append(jax.ShapeDtypeStruct((N_TOK, D_MODEL), BF16))
    return pl.pallas_call(
        functools.partial(_combine_kernel, final),
        grid=(N_TOK // CMB_TM,),
        in_specs=[tok, half, half, pl.BlockSpec((CMB_TM, TOP_K), lambda i: (i, 0)), seg, seg,
                  pl.BlockSpec((1, D_MODEL), lambda i: (0, 0))],
        out_specs=[tok] * len(out_shape),
        out_shape=out_shape,
        compiler_params=pltpu.CompilerParams(
            dimension_semantics=("arbitrary",), vmem_limit_bytes=VMEM_LIMIT),
        name="moe_combine",
    )(x_new, y0, y1, w, mods, next_mods, g)


def kernel(x_prompt, x_sample, cache_k, cache_v, state_rwkv, c, c_ctx, norm1_g, w_mod, b_mod, w_in,
           rw_shift, rw_w0, rw_w2, rw_a0, rw_a2, rw_g2, rw_kk, rw_ka, rw_rk, rw_ln_g, rw_ln_b, na_rpb,
           cv_dw, cv_dw_b, cv_ln_g, cv_ln_b, w_out, norm2_g, rt_group, rt_expert, ex_w13, ex_w2, final_g):
    x = jnp.concatenate([x_prompt.reshape(N_CTX_TOK, D_MODEL), x_sample.reshape(N_LAT_TOK, D_MODEL)], 0)
    cond = jnp.concatenate([c_ctx[None], c], 0)
    seg_cond = np.concatenate([np.zeros(N_CTX_SEG, np.int32), 1 + np.arange(DEC_BATCH, dtype=np.int32)])
    zero_state = jnp.zeros((BATCH, N_PAIR, LANES, LANES), F32)
    w_in16 = w_in.astype(BF16)
    col_blocks = lambda w, tn: jnp.swapaxes(w.reshape(DEPTH, D_MODEL, w.shape[-1] // tn, tn), 1, 2)
    w_rw16 = col_blocks(w_in16[..., :RW_IN], PROJ_RW_TN)
    w_na16 = col_blocks(w_in16[..., RW_IN:RW_IN + NA_IN], PROJ_NA_TN)
    w_cv16 = col_blocks(w_in16[..., RW_IN + NA_IN:], PROJ_CV_TN)
    w_out16 = w_out.astype(BF16)
    w13_16 = ex_w13.astype(BF16)
    w2_16 = ex_w2.astype(BF16)
    cond_pad = jnp.concatenate([cond, jnp.zeros((MOD_ROWS - 1 - DEC_BATCH, D_MODEL), F32)], 0)
    m_all = _adaln_mods(cond_pad, w_mod, b_mod[:, None, :])
    mods_all = [m_all[l].reshape(MOD_ROWS, 6, D_MODEL)[seg_cond] for l in range(DEPTH)]
    h = _norm_mod(x, mods_all[0], norm1_g[0][None])
    kc_all = cache_k.reshape(DEC_BATCH, DEPTH, PAST_LEN, C_NA)
    vc_all = cache_v.reshape(DEC_BATCH, DEPTH, PAST_LEN, C_NA)
    bias_all = _natten_bias(na_rpb)
    ks_out, vs_out, ss_out = [], [], []
    for l in range(DEPTH):
        W = dict(rw_shift=rw_shift[l], rw_w0=rw_w0[l], rw_w2=rw_w2[l], rw_a0=rw_a0[l], rw_a2=rw_a2[l],
                 rw_g2=rw_g2[l], rw_kk=rw_kk[l], rw_ka=rw_ka[l], rw_rk=rw_rk[l], rw_ln_g=rw_ln_g[l],
                 rw_ln_b=rw_ln_b[l], cv_dw=cv_dw[l], cv_dw_b=cv_dw_b[l], cv_ln_g=cv_ln_g[l],
                 cv_ln_b=cv_ln_b[l])
        mods = mods_all[l]
        p_rw, p_na, p_cv = _project(h, l, w_rw16, w_na16, w_cv16)
        y_rw_c, s_f, s_b = _rwkv_mix(p_rw, zero_state, zero_state, BATCH, SEQ, 0, W)
        y_na_c = _attention(p_na, BATCH, SEQ, 0)
        y_cv_c = _conv_module(p_cv, BATCH, SEQ, 0, W)
        s0 = state_rwkv[:, l]
        y_rw_l, _, _ = _rwkv_mix(p_rw, _state_to_tiles(s0[:, 0]), _state_to_tiles(s0[:, 1]),
                                 DEC_BATCH, DEC_SEQ, N_CTX_TOK, W)
        y_na_l = _attention(p_na, DEC_BATCH, DEC_SEQ, N_CTX_TOK, l, kc_all, vc_all, bias_all)
        y_cv_l = _conv_module(p_cv, DEC_BATCH, DEC_SEQ, N_CTX_TOK, W)
        ks_out.append(p_na[1, :N_CTX_TOK].reshape(BATCH, SEQ, H_NA, HD))
        vs_out.append(p_na[2, :N_CTX_TOK].reshape(BATCH, SEQ, H_NA, HD))
        ss_out.append(jnp.stack([_tiles_to_state(s_f), _tiles_to_state(s_b)], 1))
        rt = jnp.concatenate([rt_group[l], rt_expert[l],
                              jnp.zeros((D_MODEL, RT_PAD - N_GROUPS - N_EXPERTS), F32)], 1)
        x_new, h2, logits = _out_project((y_rw_c, y_na_c, y_cv_c), (y_rw_l, y_na_l, y_cv_l), x, mods,
                                         norm2_g[l][None], l, w_out16, rt)
        y0, y1, w_tok = _hmoe(h2, logits, l, w13_16, w2_16)
        if l == DEPTH - 1:
            (y,) = _combine(True, x_new, y0, y1, w_tok, mods, mods, final_g[None])
        else:
            x, h = _combine(False, x_new, y0, y1, w_tok, mods, mods_all[l + 1], norm1_g[l + 1][None])
    y_prompt = y[:N_CTX_TOK].reshape(BATCH, SEQ, D_MODEL)
    y_sample = y[N_CTX_TOK:].reshape(DEC_BATCH, DEC_SEQ, D_MODEL)
    return (y_prompt, y_sample, jnp.stack(ks_out, axis=1), jnp.stack(vs_out, axis=1),
            jnp.stack(ss_out, axis=1))
```

```python
import functools

import jax
import jax.numpy as jnp
import numpy as np
from jax import lax
from jax.experimental import pallas as pl
from jax.experimental.pallas import tpu as pltpu

D_MODEL = 2048
BATCH = 16
SEQ = 256
DEPTH = 4
DEC_BATCH = 8
DEC_SEQ = 1024
PAST_LEN = 512

GRID_W = 64
HD = 64
H_RW = 12
C_RW = H_RW * HD
H_NA = 12
C_NA = H_NA * HD
C_CV = D_MODEL - C_RW - C_NA
R_W = 64
R_A = 64
R_G = 128
RW_IN = 3 * C_RW + 2 * R_W + 2 * R_A + R_G
NA_IN = 3 * C_NA
CV_IN = 2 * C_CV
NA_KH_MAX = 8
NA_KW = 16
CONV_K = 31
N_GROUPS = 4
EXP_PER_GROUP = 8
N_EXPERTS = N_GROUPS * EXP_PER_GROUP
TOP_K = 2
D_EXPERT = 1024
NORM_EPS = 1e-6
LN_EPS = 1e-5
RW_GN_EPS = 64e-5
NEG = -1e30

LANES = 128
N_PAIR = H_RW // 2
SEG = 1024
N_CTX_TOK = BATCH * SEQ
N_LAT_TOK = DEC_BATCH * DEC_SEQ
N_TOK = N_CTX_TOK + N_LAT_TOK
N_SEG = N_TOK // SEG
N_CTX_SEG = N_CTX_TOK // SEG
RT_PAD = LANES
MOE_BM = 256
MOE_NBLK = (N_TOK * TOP_K) // MOE_BM + N_EXPERTS
VMEM_LIMIT = 48 * 1024 * 1024

BF16 = jnp.bfloat16
F32 = jnp.float32


def _dot(a, b):
    return jnp.dot(a, b, preferred_element_type=F32)


def _dot_t(a, b):
    return lax.dot_general(a, b, (((1,), (1,)), ((), ())), preferred_element_type=F32)


def _pack_bf16_pairs(x):
    n = x.shape[-1] // 2
    bits = lax.bitcast_convert_type(x.astype(BF16).astype(F32), jnp.int32)
    return bits[:, :n] | lax.shift_right_logical(bits[:, n:], 16)


def _unpack_bf16_pairs(packed):
    hi = lax.bitcast_convert_type(packed & jnp.int32(-65536), F32)
    lo = lax.bitcast_convert_type(lax.shift_left(packed, 16), F32)
    return jnp.concatenate([hi, lo], -1)


def _split_bf16_resid(x):
    hi = x.astype(BF16)
    return hi, x - hi.astype(F32)


def _sigmoid(x):
    return 1.0 / (1.0 + jnp.exp(-x))


def _head_sum(x, m_a):
    s_a = jnp.sum(jnp.where(m_a, x, 0.0), -1, keepdims=True)
    s_b = jnp.sum(jnp.where(m_a, 0.0, x), -1, keepdims=True)
    return jnp.where(m_a, s_a, s_b)


MOD_ROWS = 16
MOD_TN = 1024


def _mods_kernel(c_ref, w_ref, b_ref, o_ref):
    c = c_ref[...]
    o_ref[0] = _dot((c * _sigmoid(c)).astype(BF16), w_ref[0].astype(BF16)) + b_ref[0]


def _adaln_mods(cond, w_mod, b_mod):
    return pl.pallas_call(
        _mods_kernel,
        grid=(DEPTH, 6 * D_MODEL // MOD_TN),
        in_specs=[
            pl.BlockSpec((MOD_ROWS, D_MODEL), lambda l, j: (0, 0)),
            pl.BlockSpec((1, D_MODEL, MOD_TN), lambda l, j: (l, 0, j)),
            pl.BlockSpec((1, 1, MOD_TN), lambda l, j: (l, 0, j)),
        ],
        out_specs=pl.BlockSpec((1, MOD_ROWS, MOD_TN), lambda l, j: (l, 0, j)),
        out_shape=jax.ShapeDtypeStruct((DEPTH, MOD_ROWS, 6 * D_MODEL), F32),
        compiler_params=pltpu.CompilerParams(
            dimension_semantics=("arbitrary", "arbitrary"), vmem_limit_bytes=VMEM_LIMIT),
        name="adaln_mods",
    )(cond, w_mod, b_mod)


NORM_TM = 256


def _modulated_norm(x, g, shift, scale):
    y = x * lax.rsqrt(jnp.mean(x * x, -1, keepdims=True) + NORM_EPS)
    return (y * g) * (1.0 + scale) + shift


def _norm_kernel(x_ref, mods_ref, g_ref, h_ref):
    h_ref[...] = _modulated_norm(x_ref[...], g_ref[...], mods_ref[0, 0:1, :], mods_ref[0, 1:2, :]).astype(BF16)


def _norm_mod(x, mods, g):
    tok = pl.BlockSpec((NORM_TM, D_MODEL), lambda i: (i, 0))
    return pl.pallas_call(
        _norm_kernel,
        grid=(N_TOK // NORM_TM,),
        in_specs=[tok, pl.BlockSpec((1, 6, D_MODEL), lambda i: (i // (SEG // NORM_TM), 0, 0)),
                  pl.BlockSpec((1, D_MODEL), lambda i: (0, 0))],
        out_specs=tok,
        out_shape=jax.ShapeDtypeStruct((N_TOK, D_MODEL), BF16),
        compiler_params=pltpu.CompilerParams(
            dimension_semantics=("arbitrary",), vmem_limit_bytes=VMEM_LIMIT),
        name="norm_mod",
    )(x, mods, g)


PROJ_TM = SEG
PROJ_RW_TN = RW_IN // 3
PROJ_NA_TN = NA_IN // 3
PROJ_CV_TN = CV_IN // 2
PROJ_STEPS = 8


def _proj_kernel(h_ref, wrw_ref, wna_ref, wcv_ref, orw_ref, ona_ref, ocv_ref):
    j = pl.program_id(1)

    @pl.when(j < 3)
    def _():
        orw_ref[0] = _dot(h_ref[...], wrw_ref[0, 0])

    @pl.when((j >= 3) & (j < 6))
    def _():
        ona_ref[0] = _dot(h_ref[...], wna_ref[0, 0])

    @pl.when(j >= 6)
    def _():
        ocv_ref[0] = _dot(h_ref[...], wcv_ref[0, 0])


def _project(h, layer, w_rw, w_na, w_cv):
    def c_rw(j):
        return jnp.minimum(j, 2)

    def c_na(j):
        return jnp.clip(j - 3, 0, 2)

    def c_cv(j):
        return jnp.clip(j - 6, 0, 1)

    return pl.pallas_call(
        _proj_kernel,
        grid=(N_TOK // PROJ_TM, PROJ_STEPS),
        in_specs=[
            pl.BlockSpec((PROJ_TM, D_MODEL), lambda i, j: (i, 0)),
            pl.BlockSpec((1, 1, D_MODEL, PROJ_RW_TN), lambda i, j: (layer, c_rw(j), 0, 0)),
            pl.BlockSpec((1, 1, D_MODEL, PROJ_NA_TN), lambda i, j: (layer, c_na(j), 0, 0)),
            pl.BlockSpec((1, 1, D_MODEL, PROJ_CV_TN), lambda i, j: (layer, c_cv(j), 0, 0)),
        ],
        out_specs=[
            pl.BlockSpec((1, PROJ_TM, PROJ_RW_TN), lambda i, j: (c_rw(j), i, 0)),
            pl.BlockSpec((1, PROJ_TM, PROJ_NA_TN), lambda i, j: (c_na(j), i, 0)),
            pl.BlockSpec((1, PROJ_TM, PROJ_CV_TN), lambda i, j: (c_cv(j), i, 0)),
        ],
        out_shape=[
            jax.ShapeDtypeStruct((RW_IN // PROJ_RW_TN, N_TOK, PROJ_RW_TN), F32),
            jax.ShapeDtypeStruct((NA_IN // PROJ_NA_TN, N_TOK, PROJ_NA_TN), F32),
            jax.ShapeDtypeStruct((CV_IN // PROJ_CV_TN, N_TOK, PROJ_CV_TN), F32),
        ],
        compiler_params=pltpu.CompilerParams(
            dimension_semantics=("arbitrary", "arbitrary"), vmem_limit_bytes=VMEM_LIMIT),
        name="in_proj",
    )(h, w_rw, w_na, w_cv)


RW_C = 64
RW_NCH = 4
RW_BLK = RW_NCH * RW_C
RW_HALO = 8
RW_SB = 16
RW_SB_SQUARINGS = 3
assert RW_C == 4 * RW_SB


def _rwkv_kernel(reverse, final, n_blocks, *refs):
    if final:
        (p_ref, pp_ref, pn_ref, s0_ref, shift_ref, w0_ref, w2_ref, a0_ref, a2_ref, ka_ref, kkw_ref, rk_ref,
         tri_ref, g2_ref, lng_ref, lnb_ref, ysin_ref, bnin_ref, y_ref, sfin_ref, s_scr) = refs
    else:
        (p_ref, pp_ref, pn_ref, s0_ref, shift_ref, w0_ref, w2_ref, a0_ref, a2_ref, ka_ref, kkw_ref, rk_ref,
         tri_ref, ys_ref, bn_ref, sfin_ref, s_scr) = refs
    c = pl.program_id(1)
    cc = (n_blocks - 1 - c) if reverse else c

    @pl.when(c == 0)
    def _():
        s_scr[...] = s0_ref[0]

    n_cb = RW_IN // PROJ_RW_TN
    x = jnp.concatenate([p_ref[cb] for cb in range(n_cb)], -1)
    row = lax.broadcasted_iota(jnp.int32, (RW_BLK, 1), 0)
    hp = jnp.concatenate([pp_ref[cb, RW_HALO - 1:RW_HALO, :] for cb in range(n_cb)], -1)
    hn = jnp.concatenate([pn_ref[cb, 0:1, :] for cb in range(n_cb)], -1)
    hp = jnp.where(cc > 0, hp, 0.0)
    hn = jnp.where(cc < n_blocks - 1, hn, 0.0)
    prev = jnp.where(row == 0, hp, pltpu.roll(x, 1, 0))
    nxt = jnp.where(row == RW_BLK - 1, hn, pltpu.roll(x, RW_BLK - 1, 0))
    xs = x + shift_ref[0:1, :] * (prev - x) + shift_ref[1:2, :] * (nxt - x)

    r = xs[:, 0:C_RW]
    k = xs[:, C_RW:2 * C_RW]
    v = xs[:, 2 * C_RW:3 * C_RW]
    wd = xs[:, 3 * C_RW:3 * C_RW + 2 * R_W]
    ad = xs[:, 3 * C_RW + 2 * R_W:3 * C_RW + 2 * R_W + 2 * R_A]
    wl = w0_ref[...] + _dot(jnp.tanh(wd).astype(BF16), w2_ref[...])
    softplus_neg = jnp.maximum(-wl, 0.0) + jnp.log(1.0 + jnp.exp(-jnp.abs(wl)))
    lw = -jnp.exp(-softplus_neg - 0.5)
    a_sig = _sigmoid(a0_ref[...] + _dot(ad.astype(BF16), a2_ref[...]))
    kmod = k * (1.0 + (a_sig - 1.0) * ka_ref[...])
    kkf = k * kkw_ref[...]
    bonus_w = r * kmod * rk_ref[...]
    tri = tri_ref[...]
    lw_hi, lw_r = _split_bf16_resid(lw)
    lw_mid, lw_r = _split_bf16_resid(lw_r)
    cl = _dot(tri, lw_hi) + (_dot(tri, lw_mid) + _dot(tri, lw_r.astype(BF16)))
    end_rows = [j * RW_C if reverse else (j + 1) * RW_C - 1 for j in range(RW_NCH)]
    cl_ends = [cl[e:e + 1, :] for e in end_rows]
    cl_end = jnp.concatenate([jnp.broadcast_to(ce, (RW_C, C_RW)) for ce in cl_ends], 0)
    g_in = jnp.exp(cl)
    g_inv = jnp.exp(-cl)
    g_prev = jnp.exp(cl - lw)
    g_end = jnp.exp(cl_end - cl)
    g_tot = [jnp.exp(ce) for ce in cl_ends]

    lane = lax.broadcasted_iota(jnp.int32, (1, LANES), 1)
    m_a = lane < HD
    ri = lax.broadcasted_iota(jnp.int32, (2 * RW_C, 2 * RW_C), 0)
    ci = lax.broadcasted_iota(jnp.int32, (2 * RW_C, 2 * RW_C), 1)
    strict = (ri < ci) if reverse else (ri > ci)
    incl = (ri <= ci) if reverse else (ri >= ci)
    eye = ri == ci
    sub_blk = (ri // RW_SB) == (ci // RW_SB)

    def stack(t):
        return jnp.concatenate([jnp.where(m_a, t, 0.0), jnp.where(m_a, 0.0, t)], 0)

    if final:
        gd = xs[:, 3 * C_RW + 2 * R_W + 2 * R_A:]
        gate = _dot(_sigmoid(gd).astype(BF16), g2_ref[...])

    n = 2 * RW_C
    sls = [slice(p * LANES, (p + 1) * LANES) for p in range(N_PAIR)]
    rws = [slice(j * RW_C, (j + 1) * RW_C) for j in range(RW_NCH)]
    items = [(j, p) for j in range(RW_NCH) for p in range(N_PAIR)]
    pairs = range(len(items))
    z_a, z_r, z_v, z_bh, z_kh, bonus = [], [], [], [], [], []
    l_ab, l_ak, t_rb, t_rk = [], [], [], []
    for j, p in items:
        rs, sl = rws[j], sls[p]
        r_p, k_p, v_p, a_p = r[rs, sl], kmod[rs, sl], v[rs, sl], a_sig[rs, sl]
        kkf_p = kkf[rs, sl]
        kk = kkf_p * lax.rsqrt(jnp.maximum(_head_sum(kkf_p * kkf_p, m_a), 1e-24))
        bb = kk * a_p
        bonus.append(_head_sum(bonus_w[rs, sl], m_a) * v_p)
        z_a.append(stack(-kk * g_prev[rs, sl]))
        z_r.append(stack(r_p * g_in[rs, sl]))
        z_b = stack(bb * g_inv[rs, sl])
        z_k = stack(k_p * g_inv[rs, sl])
        z_bh.append(stack(bb * g_end[rs, sl]))
        z_kh.append(stack(k_p * g_end[rs, sl]))
        z_v.append(stack(v_p))
        gram = _dot_t(jnp.concatenate([z_a[-1], z_r[-1]], 0).astype(BF16),
                      jnp.concatenate([z_b, z_k], 0).astype(BF16))
        l_ab.append(jnp.where(strict, gram[:n, :n], 0.0).astype(BF16))
        l_ak.append(jnp.where(strict, gram[:n, n:], 0.0).astype(BF16))
        t_rb.append(jnp.where(incl, gram[n:, :n], 0.0))
        t_rk.append(jnp.where(incl, gram[n:, n:], 0.0))
    xx = [jnp.concatenate([z_a[p], _dot(l_ak[p], z_v[p].astype(BF16))], 1) for p in pairs]
    pw = [jnp.where(sub_blk, l_ab[p], 0.0) for p in pairs]
    n_off = [jnp.where(sub_blk, 0.0, l_ab[p]) for p in pairs]
    td = [jnp.where(eye, 1.0, pw[p].astype(F32)) for p in pairs]
    for _ in range(RW_SB_SQUARINGS):
        pw = [_dot(pw[p], pw[p]).astype(BF16) for p in pairs]
        td = [td[p] + _dot(td[p].astype(BF16), pw[p]) for p in pairs]
    td = [td[p].astype(BF16) for p in pairs]
    mm = [_dot(td[p], n_off[p]).astype(BF16) for p in pairs]
    xx = [_dot(td[p], xx[p].astype(BF16)) for p in pairs]
    xx = [xx[p] + _dot(mm[p], xx[p].astype(BF16)) for p in pairs]
    mm = [_dot(mm[p], mm[p]).astype(BF16) for p in pairs]
    xx = [xx[p] + _dot(mm[p], xx[p].astype(BF16)) for p in pairs]
    res = []
    for p in pairs:
        lhs = jnp.concatenate([jnp.concatenate([t_rb[p], t_rk[p]], 1),
                               jnp.concatenate([z_bh[p].T, z_kh[p].T], 1)], 0).astype(BF16)
        rhs = jnp.concatenate([xx[p], jnp.concatenate([jnp.zeros((n, LANES), F32), z_v[p]], 1)], 0).astype(BF16)
        res.append(_dot(lhs, rhs))
    state = [s_scr[p] for p in range(N_PAIR)]
    for j in (reversed(range(RW_NCH)) if reverse else range(RW_NCH)):
        for p in range(N_PAIR):
            i, rs, sl = j * N_PAIR + p, rws[j], sls[p]
            q_m = z_r[i] + res[i][:n, :LANES]
            g_m = jnp.where(eye, g_tot[j][:, sl], 0.0) + res[i][n:, :LANES]
            upd = _dot(jnp.concatenate([q_m, g_m], 0).astype(BF16), state[p].astype(BF16))
            ys = upd[:n] + res[i][:n, LANES:]
            state[p] = upd[n:] + res[i][n:, LANES:]
            y_p = ys[:RW_C] + ys[RW_C:]
            if final:
                y_t = y_p + ysin_ref[rs, sl]
                mu = _head_sum(y_t, m_a) * (1.0 / HD)
                d = y_t - mu
                var = _head_sum(d * d, m_a) * (1.0 / HD)
                y_n = d * lax.rsqrt(var + RW_GN_EPS) * lng_ref[:, sl] + lnb_ref[:, sl]
                y_ref[rs, sl] = ((y_n + bonus[i] + bnin_ref[rs, sl]) * gate[rs, sl]).astype(y_ref.dtype)
            else:
                ys_ref[rs, sl] = y_p
                bn_ref[rs, sl] = bonus[i]
    for p in range(N_PAIR):
        s_scr[p] = state[p]

    @pl.when(c == n_blocks - 1)
    def _():
        sfin_ref[0] = s_scr[...]


def _rwkv_pass(reverse, final, batch, seq, base_row, p_rw, s0, wts, extra):
    n_blocks = seq // RW_BLK
    n_cb = RW_IN // PROJ_RW_TN
    base_blk = base_row // RW_BLK
    halo_per_chunk = RW_BLK // RW_HALO
    n_halo_blk = N_TOK // RW_HALO

    def chunk(c):
        return (n_blocks - 1 - c) if reverse else c

    def row_blk(b, c):
        return base_blk + b * n_blocks + chunk(c)

    def prev_blk(b, c):
        return jnp.maximum(row_blk(b, c) * halo_per_chunk - 1, 0)

    def next_blk(b, c):
        return jnp.minimum((row_blk(b, c) + 1) * halo_per_chunk, n_halo_blk - 1)

    full = lambda shape: pl.BlockSpec(shape, lambda b, c: (0,) * len(shape))
    tok = lambda width: pl.BlockSpec((RW_BLK, width), lambda b, c: (row_blk(b, c), 0))
    loc = lambda width: pl.BlockSpec((RW_BLK, width), lambda b, c: (b * n_blocks + chunk(c), 0))
    state = pl.BlockSpec((1, N_PAIR, LANES, LANES), lambda b, c: (b, 0, 0, 0))
    in_specs = [
        pl.BlockSpec((n_cb, RW_BLK, PROJ_RW_TN), lambda b, c: (0, row_blk(b, c), 0)),
        pl.BlockSpec((n_cb, RW_HALO, PROJ_RW_TN), lambda b, c: (0, prev_blk(b, c), 0)),
        pl.BlockSpec((n_cb, RW_HALO, PROJ_RW_TN), lambda b, c: (0, next_blk(b, c), 0)),
        state,
        full((2, RW_IN)), full((1, C_RW)), full((2 * R_W, C_RW)), full((1, C_RW)), full((2 * R_A, C_RW)),
        full((1, C_RW)), full((1, C_RW)), full((1, C_RW)), full((RW_BLK, RW_BLK)),
    ]
    args = [p_rw, p_rw, p_rw, s0] + list(wts)
    if final:
        in_specs += [full((R_G, C_RW)), full((1, C_RW)), full((1, C_RW)), loc(C_RW), loc(C_RW)]
        args += list(extra)
        out_specs = [loc(C_RW), state]
        out_shape = [jax.ShapeDtypeStruct((batch * seq, C_RW), BF16),
                     jax.ShapeDtypeStruct((batch, N_PAIR, LANES, LANES), F32)]
    else:
        out_specs = [loc(C_RW), loc(C_RW), state]
        out_shape = [jax.ShapeDtypeStruct((batch * seq, C_RW), F32), jax.ShapeDtypeStruct((batch * seq, C_RW), F32),
                     jax.ShapeDtypeStruct((batch, N_PAIR, LANES, LANES), F32)]
    return pl.pallas_call(
        functools.partial(_rwkv_kernel, reverse, final, n_blocks),
        grid=(batch, n_blocks),
        in_specs=in_specs,
        out_specs=out_specs,
        out_shape=out_shape,
        scratch_shapes=[pltpu.VMEM((N_PAIR, LANES, LANES), F32)],
        compiler_params=pltpu.CompilerParams(
            dimension_semantics=("arbitrary", "arbitrary"), vmem_limit_bytes=VMEM_LIMIT),
        name="rwkv_bwd" if reverse else "rwkv_fwd",
    )(*args)


def _state_to_tiles(s):
    b = s.shape[0]
    st = jnp.swapaxes(s, -1, -2).reshape(b, N_PAIR, 2, HD, HD)
    z = jnp.zeros((b, N_PAIR, HD, HD), s.dtype)
    top = jnp.concatenate([st[:, :, 0], z], -1)
    bot = jnp.concatenate([z, st[:, :, 1]], -1)
    return jnp.concatenate([top, bot], -2)


def _tiles_to_state(t):
    b = t.shape[0]
    s_a = t[:, :, :HD, :HD]
    s_b = t[:, :, HD:, HD:]
    st = jnp.stack([s_a, s_b], 2).reshape(b, H_RW, HD, HD)
    return jnp.swapaxes(st, -1, -2)


def _rwkv_mix(p_rw, s0_fwd, s0_bwd, batch, seq, base_row, W):
    def dir_weights(d):
        pad = lambda m: jnp.zeros((2 * m.shape[1], C_RW), BF16).at[d * m.shape[1]:(d + 1) * m.shape[1]].set(
            m[d].astype(BF16))
        tri = np.triu(np.ones((RW_C, RW_C), np.float32)) if d == 1 else np.tril(np.ones((RW_C, RW_C), np.float32))
        tri = np.kron(np.eye(RW_NCH, dtype=np.float32), tri)
        return [W['rw_shift'], W['rw_w0'][d][None], pad(W['rw_w2']), W['rw_a0'][d][None], pad(W['rw_a2']),
                W['rw_ka'][None], W['rw_kk'][None], W['rw_rk'].reshape(1, C_RW), jnp.asarray(tri, BF16)]

    ys_b, bn_b, s_b = _rwkv_pass(True, False, batch, seq, base_row, p_rw, s0_bwd, dir_weights(1), None)
    extra = [W['rw_g2'].astype(BF16), W['rw_ln_g'][None], W['rw_ln_b'][None], ys_b, bn_b]
    y, s_f = _rwkv_pass(False, True, batch, seq, base_row, p_rw, s0_fwd, dir_weights(0), extra)
    return y, s_f, s_b


ATT_TQ = 256
ATT_CTX_GROUP = 4


def _window_rows(i):
    rows = DEC_SEQ // GRID_W
    kh = min(NA_KH_MAX, rows)
    return min(max(i - kh // 2, 0), rows - kh), kh


def _window_keys(t):
    q_rows = range(t * ATT_TQ // GRID_W, (t + 1) * ATT_TQ // GRID_W)
    lo = min(_window_rows(i)[0] for i in q_rows)
    hi = max(sum(_window_rows(i)) for i in q_rows)
    per_tile = LANES // GRID_W
    lo, hi = lo // per_tile * per_tile, -(-hi // per_tile) * per_tile
    return slice(lo * GRID_W, hi * GRID_W)


def _attn_kernel(n_q, n_sub, has_ctx, *refs):
    if has_ctx:
        q_ref, k_ref, v_ref, kc_ref, vc_ref, bias_ref, o_ref = refs
    else:
        q_ref, k_ref, v_ref, o_ref = refs
    lane = lax.broadcasted_iota(jnp.int32, (1, LANES), 1)
    m_a = lane < HD
    if has_ctx:
        kc = kc_ref[0, 0].astype(BF16)
        vc = vc_ref[0, 0].astype(BF16)
    kv = [(k_ref[0, g * n_q:(g + 1) * n_q, :].astype(BF16), v_ref[0, g * n_q:(g + 1) * n_q, :].astype(BF16))
          for g in range(n_sub)]
    for g, t in [(g, t) for g in range(n_sub) for t in range(n_q // ATT_TQ)]:
        k, v = kv[g]
        rows = slice(t * ATT_TQ, (t + 1) * ATT_TQ)
        out_rows = slice(g * n_q + t * ATT_TQ, g * n_q + (t + 1) * ATT_TQ)
        q = q_ref[0, out_rows, :] * (HD ** -0.5)
        keys = _window_keys(t) if has_ctx else slice(0, n_q)
        k_t, v_t = k[keys, :], v[keys, :]
        outs = []
        for h in range(2):
            q_h = jnp.where(m_a if h == 0 else jnp.logical_not(m_a), q, 0.0).astype(BF16)
            s = _dot_t(q_h, k_t)
            if has_ctx:
                s = s + bias_ref[0, h, rows, keys]
                s_c = _dot_t(q_h, kc)
                m = jnp.maximum(jnp.max(s, -1, keepdims=True), jnp.max(s_c, -1, keepdims=True))
                e_c = jnp.exp(s_c - m)
                e = jnp.exp(s - m)
                den = jnp.sum(e, -1, keepdims=True) + jnp.sum(e_c, -1, keepdims=True)
                o = _dot(e.astype(BF16), v_t) + _dot(e_c.astype(BF16), vc)
            else:
                m = jnp.max(s, -1, keepdims=True)
                e = jnp.exp(s - m)
                den = jnp.sum(e, -1, keepdims=True)
                o = _dot(e.astype(BF16), v_t)
            outs.append(o / den)
        o_ref[out_rows, :] = jnp.where(m_a, outs[0], outs[1]).astype(o_ref.dtype)


def _attention(p_na, batch, seq, base_row, layer=None, kc=None, vc=None, bias=None):
    has_ctx = kc is not None
    n_sub = 1 if has_ctx else ATT_CTX_GROUP
    blk = n_sub * seq
    base_blk = base_row // blk
    tok = lambda part: pl.BlockSpec((1, blk, LANES), lambda p, b: (part, base_blk + b, p))
    in_specs = [tok(0), tok(1), tok(2)]
    args = [p_na, p_na, p_na]
    if has_ctx:
        cache = pl.BlockSpec((1, 1, PAST_LEN, LANES), lambda p, b: (b, layer, 0, p))
        in_specs += [cache, cache, pl.BlockSpec((1, 2, seq, seq), lambda p, b: (layer, p, 0, 0))]
        args += [kc, vc, bias]
    return pl.pallas_call(
        functools.partial(_attn_kernel, seq, n_sub, has_ctx),
        grid=(N_PAIR, batch // n_sub),
        in_specs=in_specs,
        out_specs=pl.BlockSpec((blk, LANES), lambda p, b: (b, p)),
        out_shape=jax.ShapeDtypeStruct((batch * seq, C_NA), BF16),
        compiler_params=pltpu.CompilerParams(
            dimension_semantics=("arbitrary", "arbitrary"), vmem_limit_bytes=VMEM_LIMIT),
        name="natten_lat" if has_ctx else "attn_ctx",
    )(*args)


def _natten_bias(rpb):
    rows = DEC_SEQ // GRID_W
    kh = min(NA_KH_MAX, rows)
    qi = np.arange(rows)
    si = np.clip(qi - kh // 2, 0, rows - kh)
    ok_r = (qi[None, :] >= si[:, None]) & (qi[None, :] < si[:, None] + kh)
    idx_r = np.clip(qi[None, :] - qi[:, None] + NA_KH_MAX - 1, 0, 2 * NA_KH_MAX - 2)
    qj = np.arange(GRID_W)
    wj = np.clip(qj - NA_KW // 2, 0, GRID_W - NA_KW)
    ok_c = (qj[None, :] >= wj[:, None]) & (qj[None, :] < wj[:, None] + NA_KW)
    pad = GRID_W - NA_KW
    lead = rpb.shape[:-2]
    rp = jnp.pad(rpb, ((0, 0),) * (rpb.ndim - 1) + ((pad, pad),))
    toep = jnp.stack([rp[..., pad + NA_KW - 1 - j:pad + NA_KW - 1 - j + GRID_W] for j in range(GRID_W)], -2)
    toep = jnp.where(jnp.asarray(ok_c), toep, NEG)
    neg_blk = jnp.full(lead + (GRID_W, GRID_W), NEG, rpb.dtype)
    block_rows = [jnp.concatenate([toep[..., idx_r[i, k], :, :] if ok_r[i, k] else neg_blk for k in range(rows)], -1)
                  for i in range(rows)]
    return jnp.stack(block_rows, -3).reshape(lead + (DEC_SEQ, DEC_SEQ))


CV_PAD = 16
CV_TT = 256


def _conv_kernel(seq, val_ref, gate_ref, dw_ref, dwb_ref, lng_ref, lnb_ref, o_ref, u_scr, c_scr):
    zeros = jnp.zeros((CV_PAD, C_CV), F32)
    u_scr[0:CV_PAD, :] = zeros
    u_scr[seq + CV_PAD:seq + 2 * CV_PAD, :] = zeros
    u_scr[CV_PAD:seq + CV_PAD, :] = val_ref[0] * _sigmoid(gate_ref[0])
    off = CV_PAD - CONV_K // 2
    for cb in range(C_CV // LANES):
        cols = slice(cb * LANES, (cb + 1) * LANES)
        for t in range(seq // CV_TT):
            acc = jnp.zeros((CV_TT, LANES), F32)
            for j in range(CONV_K):
                acc = acc + dw_ref[j:j + 1, cols] * u_scr[t * CV_TT + off + j:t * CV_TT + off + j + CV_TT, cols]
            c_scr[t * CV_TT:(t + 1) * CV_TT, cols] = acc
    u = c_scr[...] + dwb_ref[...]
    mu = jnp.mean(u, -1, keepdims=True)
    d = u - mu
    var = jnp.mean(d * d, -1, keepdims=True)
    y = d * lax.rsqrt(var + LN_EPS) * lng_ref[...] + lnb_ref[...]
    o_ref[...] = (y * _sigmoid(y)).astype(o_ref.dtype)


def _conv_module(p_cv, batch, seq, base_row, W):
    base_blk = base_row // seq
    full = lambda shape: pl.BlockSpec(shape, lambda b: (0,) * len(shape))
    return pl.pallas_call(
        functools.partial(_conv_kernel, seq),
        grid=(batch,),
        in_specs=[
            pl.BlockSpec((1, seq, C_CV), lambda b: (0, base_blk + b, 0)),
            pl.BlockSpec((1, seq, C_CV), lambda b: (1, base_blk + b, 0)),
            full((CONV_K, C_CV)), full((1, C_CV)), full((1, C_CV)), full((1, C_CV)),
        ],
        out_specs=pl.BlockSpec((seq, C_CV), lambda b: (b, 0)),
        out_shape=jax.ShapeDtypeStruct((batch * seq, C_CV), BF16),
        scratch_shapes=[pltpu.VMEM((seq + 2 * CV_PAD, C_CV), F32), pltpu.VMEM((seq, C_CV), F32)],
        compiler_params=pltpu.CompilerParams(
            dimension_semantics=("arbitrary",), vmem_limit_bytes=VMEM_LIMIT),
        name="conv_module",
    )(p_cv, p_cv, W['cv_dw'], W['cv_dw_b'][None], W['cv_ln_g'][None], W['cv_ln_b'][None])


OUT_TM = 512
OUT_CTX_STEPS = N_CTX_TOK // OUT_TM


def _split_bf16(x):
    hi = x.astype(BF16)
    return hi, (x - hi.astype(F32)).astype(BF16)


def _out_kernel(yrw_c, yna_c, ycv_c, yrw_l, yna_l, ycv_l, x_ref, mods_ref, g_ref, w_ref, rt_ref,
                xo_ref, h_ref, lg_ref, y_scr):
    i = pl.program_id(0)

    @pl.when(i < OUT_CTX_STEPS)
    def _():
        y_scr[...] = jnp.concatenate([yrw_c[...], yna_c[...], ycv_c[...]], -1)

    @pl.when(i >= OUT_CTX_STEPS)
    def _():
        y_scr[...] = jnp.concatenate([yrw_l[...], yna_l[...], ycv_l[...]], -1)

    x_new = x_ref[...] + mods_ref[0, 2:3, :] * _dot(y_scr[...], w_ref[0])
    xo_ref[...] = x_new
    h = _modulated_norm(x_new, g_ref[...], mods_ref[0, 3:4, :], mods_ref[0, 4:5, :])
    h_hi, h_lo = _split_bf16(h)
    h_ref[...] = _pack_bf16_pairs(h)
    rt_hi, rt_lo = _split_bf16(rt_ref[...])
    lg_ref[...] = _dot(h_hi, rt_hi) + (_dot(h_hi, rt_lo) + _dot(h_lo, rt_hi))


def _out_project(y_ctx, y_lat, x, mods, g, layer, w_bf16, rt):
    per_seg = SEG // OUT_TM
    tok = lambda width: pl.BlockSpec((OUT_TM, width), lambda i: (i, 0))
    ctx = lambda width: pl.BlockSpec((OUT_TM, width), lambda i: (jnp.minimum(i, OUT_CTX_STEPS - 1), 0))
    lat = lambda width: pl.BlockSpec((OUT_TM, width), lambda i: (jnp.maximum(i - OUT_CTX_STEPS, 0), 0))
    return pl.pallas_call(
        _out_kernel,
        grid=(N_TOK // OUT_TM,),
        in_specs=[
            ctx(C_RW), ctx(C_NA), ctx(C_CV), lat(C_RW), lat(C_NA), lat(C_CV), tok(D_MODEL),
            pl.BlockSpec((1, 6, D_MODEL), lambda i: (i // per_seg, 0, 0)),
            pl.BlockSpec((1, D_MODEL), lambda i: (0, 0)),
            pl.BlockSpec((1, D_MODEL, D_MODEL), lambda i: (layer, 0, 0)),
            pl.BlockSpec((D_MODEL, RT_PAD), lambda i: (0, 0)),
        ],
        out_specs=[tok(D_MODEL), tok(D_MODEL // 2), tok(RT_PAD)],
        out_shape=[
            jax.ShapeDtypeStruct((N_TOK, D_MODEL), F32),
            jax.ShapeDtypeStruct((N_TOK, D_MODEL // 2), jnp.int32),
            jax.ShapeDtypeStruct((N_TOK, RT_PAD), F32),
        ],
        scratch_shapes=[pltpu.VMEM((OUT_TM, D_MODEL), BF16)],
        compiler_params=pltpu.CompilerParams(
            dimension_semantics=("arbitrary",), vmem_limit_bytes=VMEM_LIMIT),
        name="out_proj",
    )(*y_ctx, *y_lat, x, mods, g, w_bf16, rt)


def _moe_kernel(blk_e_ref, n_used_ref, xb_ref, w13_ref, w2_ref, o_ref):
    i = pl.program_id(0)

    @pl.when(i < n_used_ref[0])
    def _():
        xb = _unpack_bf16_pairs(xb_ref[...]).astype(BF16)
        gu = _dot(xb, w13_ref[0, 0])
        gt = gu[:, :D_EXPERT]
        up = gu[:, D_EXPERT:]
        a = (gt * _sigmoid(gt) * up).astype(BF16)
        o_ref[...] = _pack_bf16_pairs(_dot(a, w2_ref[0, 0]))

    @pl.when(i >= n_used_ref[0])
    def _():
        o_ref[...] = jnp.zeros_like(o_ref)


def _moe_blocks(blk_e, n_used, xb, layer, w13_bf16, w2_bf16):
    grid_spec = pltpu.PrefetchScalarGridSpec(
        num_scalar_prefetch=2,
        grid=(MOE_NBLK,),
        in_specs=[
            pl.BlockSpec((MOE_BM, D_MODEL // 2), lambda i, be, nu: (i, 0)),
            pl.BlockSpec((1, 1, D_MODEL, 2 * D_EXPERT), lambda i, be, nu: (layer, be[i], 0, 0)),
            pl.BlockSpec((1, 1, D_EXPERT, D_MODEL), lambda i, be, nu: (layer, be[i], 0, 0)),
        ],
        out_specs=pl.BlockSpec((MOE_BM, D_MODEL // 2), lambda i, be, nu: (i, 0)),
    )
    return pl.pallas_call(
        _moe_kernel,
        grid_spec=grid_spec,
        out_shape=jax.ShapeDtypeStruct((MOE_NBLK * MOE_BM, D_MODEL // 2), jnp.int32),
        compiler_params=pltpu.CompilerParams(
            dimension_semantics=("arbitrary",), vmem_limit_bytes=VMEM_LIMIT),
        name="moe_blocks",
    )(blk_e, n_used, xb, w13_bf16, w2_bf16)


def _hmoe(h_bf16, logits, layer, w13_bf16, w2_bf16):
    n = N_TOK
    lg = logits[:, :N_GROUPS]
    grp = jnp.argmax(lg, -1).astype(jnp.int32)
    gate_g = jnp.take_along_axis(jax.nn.softmax(lg, -1), grp[:, None], -1)
    le = logits[:, N_GROUPS:N_GROUPS + N_EXPERTS].reshape(n, N_GROUPS, EXP_PER_GROUP)
    le = jnp.take_along_axis(le, grp[:, None, None], 1)[:, 0]
    top_l, top_i = lax.top_k(le, TOP_K)
    w = gate_g * jax.nn.softmax(top_l, -1)
    eid = grp[:, None] * EXP_PER_GROUP + top_i.astype(jnp.int32)

    a_tot = n * TOP_K
    rows = MOE_NBLK * MOE_BM
    flat_e = eid.reshape(-1)
    flat_t = jnp.arange(a_tot, dtype=jnp.int32) // TOP_K
    onehot = (flat_e[:, None] == jnp.arange(N_EXPERTS, dtype=jnp.int32)[None]).astype(jnp.int32)
    csum = jnp.cumsum(onehot, axis=0)
    counts = csum[-1]
    rank = jnp.take_along_axis(csum, flat_e[:, None], 1)[:, 0] - 1
    pcounts = (counts + MOE_BM - 1) // MOE_BM * MOE_BM
    pends = jnp.cumsum(pcounts)
    pstarts = pends - pcounts
    dest = pstarts[flat_e] + rank
    row_tok = jnp.zeros((rows,), jnp.int32).at[dest].set(flat_t)
    blk_start = jnp.arange(MOE_NBLK, dtype=jnp.int32) * MOE_BM
    n_used = (pends[-1] // MOE_BM).astype(jnp.int32)
    blk_e = jnp.searchsorted(pends, blk_start, side='right').astype(jnp.int32)
    last_e = jnp.searchsorted(pends, pends[-1] - 1, side='right').astype(jnp.int32)
    blk_e = jnp.minimum(blk_e, last_e)
    xb = h_bf16.at[row_tok].get(mode="promise_in_bounds")
    yb = _moe_blocks(blk_e, n_used.reshape(1), xb, layer, w13_bf16, w2_bf16)
    d2 = dest.reshape(n, TOP_K)
    take = lambda idx: yb.at[idx].get(mode="promise_in_bounds")
    return take(d2[:, 0]), take(d2[:, 1]), w


CMB_TM = 256


def _combine_kernel(final, x_ref, y0_ref, y1_ref, w_ref, mods_ref, nmods_ref, g_ref, *o_refs):
    moe = _unpack_bf16_pairs(y0_ref[...]) * w_ref[:, 0:1] + _unpack_bf16_pairs(y1_ref[...]) * w_ref[:, 1:2]
    x = x_ref[...] + mods_ref[0, 5:6, :] * moe
    if final:
        o_refs[0][...] = x * lax.rsqrt(jnp.mean(x * x, -1, keepdims=True) + NORM_EPS) * g_ref[...]
    else:
        o_refs[0][...] = x
        o_refs[1][...] = _modulated_norm(x, g_ref[...], nmods_ref[0, 0:1, :], nmods_ref[0, 1:2, :]).astype(BF16)


def _combine(final, x_new, y0, y1, w, mods, next_mods, g):
    per_seg = SEG // CMB_TM
    tok = pl.BlockSpec((CMB_TM, D_MODEL), lambda i: (i, 0))
    seg = pl.BlockSpec((1, 6, D_MODEL), lambda i: (i // per_seg, 0, 0))
    half = pl.BlockSpec((CMB_TM, D_MODEL // 2), lambda i: (i, 0))
    out_shape = [jax.ShapeDtypeStruct((N_TOK, D_MODEL), F32)]
    if not final:
        out_shape.append(jax.ShapeDtypeStruct((N_TOK, D_MODEL), BF16))
    return pl.pallas_call(
        functools.partial(_combine_kernel, final),
        grid=(N_TOK // CMB_TM,),
        in_specs=[tok, half, half, pl.BlockSpec((CMB_TM, TOP_K), lambda i: (i, 0)), seg, seg,
                  pl.BlockSpec((1, D_MODEL), lambda i: (0, 0))],
        out_specs=[tok] * len(out_shape),
        out_shape=out_shape,
        compiler_params=pltpu.CompilerParams(
            dimension_semantics=("arbitrary",), vmem_limit_bytes=VMEM_LIMIT),
        name="moe_combine",
    )(x_new, y0, y1, w, mods, next_mods, g)


def kernel(x_prompt, x_sample, cache_k, cache_v, state_rwkv, c, c_ctx, norm1_g, w_mod, b_mod, w_in,
           rw_shift, rw_w0, rw_w2, rw_a0, rw_a2, rw_g2, rw_kk, rw_ka, rw_rk, rw_ln_g, rw_ln_b, na_rpb,
           cv_dw, cv_dw_b, cv_ln_g, cv_ln_b, w_out, norm2_g, rt_group, rt_expert, ex_w13, ex_w2, final_g):
    x = jnp.concatenate([x_prompt.reshape(N_CTX_TOK, D_MODEL), x_sample.reshape(N_LAT_TOK, D_MODEL)], 0)
    cond = jnp.concatenate([c_ctx[None], c], 0)
    seg_cond = np.concatenate([np.zeros(N_CTX_SEG, np.int32), 1 + np.arange(DEC_BATCH, dtype=np.int32)])
    zero_state = jnp.zeros((BATCH, N_PAIR, LANES, LANES), F32)
    w_in16 = w_in.astype(BF16)
    col_blocks = lambda w, tn: jnp.swapaxes(w.reshape(DEPTH, D_MODEL, w.shape[-1] // tn, tn), 1, 2)
    w_rw16 = col_blocks(w_in16[..., :RW_IN], PROJ_RW_TN)
    w_na16 = col_blocks(w_in16[..., RW_IN:RW_IN + NA_IN], PROJ_NA_TN)
    w_cv16 = col_blocks(w_in16[..., RW_IN + NA_IN:], PROJ_CV_TN)
    w_out16 = w_out.astype(BF16)
    w13_16 = ex_w13.astype(BF16)
    w2_16 = ex_w2.astype(BF16)
    cond_pad = jnp.concatenate([cond, jnp.zeros((MOD_ROWS - 1 - DEC_BATCH, D_MODEL), F32)], 0)
    m_all = _adaln_mods(cond_pad, w_mod, b_mod[:, None, :])
    mods_all = [m_all[l].reshape(MOD_ROWS, 6, D_MODEL)[seg_cond] for l in range(DEPTH)]
    h = _norm_mod(x, mods_all[0], norm1_g[0][None])
    kc_all = cache_k.reshape(DEC_BATCH, DEPTH, PAST_LEN, C_NA)
    vc_all = cache_v.reshape(DEC_BATCH, DEPTH, PAST_LEN, C_NA)
    bias_all = _natten_bias(na_rpb)
    ks_out, vs_out, ss_out = [], [], []
    for l in range(DEPTH):
        W = dict(rw_shift=rw_shift[l], rw_w0=rw_w0[l], rw_w2=rw_w2[l], rw_a0=rw_a0[l], rw_a2=rw_a2[l],
                 rw_g2=rw_g2[l], rw_kk=rw_kk[l], rw_ka=rw_ka[l], rw_rk=rw_rk[l], rw_ln_g=rw_ln_g[l],
                 rw_ln_b=rw_ln_b[l], cv_dw=cv_dw[l], cv_dw_b=cv_dw_b[l], cv_ln_g=cv_ln_g[l],
                 cv_ln_b=cv_ln_b[l])
        mods = mods_all[l]
        p_rw, p_na, p_cv = _project(h, l, w_rw16, w_na16, w_cv16)
        y_rw_c, s_f, s_b = _rwkv_mix(p_rw, zero_state, zero_state, BATCH, SEQ, 0, W)
        y_na_c = _attention(p_na, BATCH, SEQ, 0)
        y_cv_c = _conv_module(p_cv, BATCH, SEQ, 0, W)
        s0 = state_rwkv[:, l]
        y_rw_l, _, _ = _rwkv_mix(p_rw, _state_to_tiles(s0[:, 0]), _state_to_tiles(s0[:, 1]),
                                 DEC_BATCH, DEC_SEQ, N_CTX_TOK, W)
        y_na_l = _attention(p_na, DEC_BATCH, DEC_SEQ, N_CTX_TOK, l, kc_all, vc_all, bias_all)
        y_cv_l = _conv_module(p_cv, DEC_BATCH, DEC_SEQ, N_CTX_TOK, W)
        ks_out.append(p_na[1, :N_CTX_TOK].reshape(BATCH, SEQ, H_NA, HD))
        vs_out.append(p_na[2, :N_CTX_TOK].reshape(BATCH, SEQ, H_NA, HD))
        ss_out.append(jnp.stack([_tiles_to_state(s_f), _tiles_to_state(s_b)], 1))
        rt = jnp.concatenate([rt_group[l], rt_expert[l],
                              jnp.zeros((D_MODEL, RT_PAD - N_GROUPS - N_EXPERTS), F32)], 1)
        x_new, h2, logits = _out_project((y_rw_c, y_na_c, y_cv_c), (y_rw_l, y_na_l, y_cv_l), x, mods,
                                         norm2_g[l][None], l, w_out16, rt)
        y0, y1, w_tok = _hmoe(h2, logits, l, w13_16, w2_16)
        if l == DEPTH - 1:
            (y,) = _combine(True, x_new, y0, y1, w_tok, mods, mods, final_g[None])
        else:
            x, h = _combine(False, x_new, y0, y1, w_tok, mods, mods_all[l + 1], norm1_g[l + 1][None])
    y_prompt = y[:N_CTX_TOK].reshape(BATCH, SEQ, D_MODEL)
    y_sample = y[N_CTX_TOK:].reshape(DEC_BATCH, DEC_SEQ, D_MODEL)
    return (y_prompt, y_sample, jnp.stack(ks_out, axis=1), jnp.stack(vs_out, axis=1),
            jnp.stack(ss_out, axis=1))
```
